```python
import math
import jax, jax.numpy as jnp
from jax import lax
import numpy as np

D_MODEL = 4096
BATCH = 4
SEQ = 2048
DEPTH = 4
DEC_BATCH = 8
DEC_SEQ = 8
PAST_LEN = 8192
PAGE_SIZE = 128

F32 = jnp.float32
PLE_DIM = 256
D_FF = 4 * D_MODEL
HEAD_DIM = 128
DIFF_HEADS = D_MODEL // (2 * HEAD_DIM)
FOX_HEADS = D_MODEL // HEAD_DIM
GDN_HEADS = D_MODEL // HEAD_DIM
GDN_CONV = 4
GDN_CHUNK = 64
POOL_WINDOWS = (2, 4, 8, 16)
POOL_GROUPS = len(POOL_WINDOWS)
POOL_KEEP = max(POOL_WINDOWS) - 1
Q_BLOCK = 128
ROPE_THETA = 10000.0
N_MIXERS = 4
LN_EPS = 1e-5
RMS_EPS = 1e-5

kernel_name = 'hybrid_diff_fox_pool_gdn_step'


def layer_norm(x, g, b):
    xf = x.astype(F32)
    mu = jnp.mean(xf, axis=-1, keepdims=True)
    var = jnp.mean(jnp.square(xf - mu), axis=-1, keepdims=True)
    return ((xf - mu) * lax.rsqrt(var + LN_EPS) * g.astype(F32) + b.astype(F32)).astype(x.dtype)


def rms_norm(x, w):
    xf = x.astype(F32)
    return (xf * lax.rsqrt(jnp.mean(jnp.square(xf), axis=-1, keepdims=True) + RMS_EPS) * w.astype(F32)).astype(x.dtype)


def l2norm(x):
    return x * lax.rsqrt(jnp.sum(jnp.square(x), axis=-1, keepdims=True) + 1e-6)


def rope(x, pos):
    d = x.shape[-1]
    inv = ROPE_THETA ** (-jnp.arange(0, d, 2, dtype=F32) / d)
    ang = pos.astype(F32)[:, None] * inv[None, :]
    cos = jnp.cos(ang)[:, None, :]
    sin = jnp.sin(ang)[:, None, :]
    xf = x.astype(F32)
    x1, x2 = xf[..., : d // 2], xf[..., d // 2:]
    return jnp.concatenate([x1 * cos - x2 * sin, x2 * cos + x1 * sin], axis=-1).astype(x.dtype)


def gather_pages(pool, page_table):
    g = pool[page_table]
    return g.reshape(g.shape[0], g.shape[1] * g.shape[2], *g.shape[3:])


def sweep_query_blocks(fn, q_arrays, q_pos):
    L = q_pos.shape[0]
    qb = min(Q_BLOCK, L)
    nb = L // qb
    blocks = tuple(jnp.swapaxes(a.reshape(a.shape[0], nb, qb, *a.shape[2:]), 0, 1) for a in q_arrays)
    out = lax.map(lambda args: fn(*args), blocks + (q_pos.reshape(nb, qb),))
    out = jnp.swapaxes(out, 0, 1)
    return out.reshape(out.shape[0], L, *out.shape[3:])


def diff_attention(x, start, past, w_qkv, lam_q1, lam_k1, lam_q2, lam_k2, subln_w, w_o, lambda_init):
    B, L, _ = x.shape
    H, dh = DIFF_HEADS, HEAD_DIM
    pos = start + jnp.arange(L)
    q, k, v = jnp.split(x @ w_qkv, 3, axis=-1)
    q = rope(q.reshape(B, L, 2 * H, dh), pos)
    k = rope(k.reshape(B, L, 2 * H, dh), pos)
    v = v.reshape(B, L, H, 2 * dh)
    if past is None:
        k_all, v_all = k, v
    else:
        k_all = jnp.concatenate([past[0].astype(k.dtype), k], axis=1)
        v_all = jnp.concatenate([past[1].astype(v.dtype), v], axis=1)
    Lk = k_all.shape[1]
    k_pos = jnp.arange(Lk)
    kh = k_all.reshape(B, Lk, H, 2, dh)
    lam = (jnp.exp(jnp.sum((lam_q1 * lam_k1).astype(F32)))
           - jnp.exp(jnp.sum((lam_q2 * lam_k2).astype(F32))) + lambda_init)
    scale = dh ** -0.5

    def block(qb, qpos):
        qh = qb.reshape(B, qb.shape[1], H, 2, dh)
        s = jnp.einsum('bqhcd,bkhcd->bhcqk', qh, kh).astype(F32) * scale
        s = jnp.where(k_pos[None, :] <= qpos[:, None], s, -jnp.inf)
        p = jax.nn.softmax(s, axis=-1)
        a = p[:, :, 0] - lam * p[:, :, 1]
        return jnp.einsum('bhqk,bkhe->bqhe', a.astype(v_all.dtype), v_all)

    o = sweep_query_blocks(block, (q,), pos)
    o = rms_norm(o, subln_w) * (1.0 - lambda_init)
    y = o.reshape(B, L, H * 2 * dh) @ w_o
    return y, k, v


def fox_attention(x, start, past, w_in, b_f, w_o):
    B, L, _ = x.shape
    H, dh = FOX_HEADS, HEAD_DIM
    D = H * dh
    pos = start + jnp.arange(L)
    proj = x @ w_in
    q = proj[..., :D].reshape(B, L, H, dh)
    k = proj[..., D:2 * D].reshape(B, L, H, dh)
    v = proj[..., 2 * D:3 * D].reshape(B, L, H, dh)
    logf = jax.nn.log_sigmoid(proj[..., 3 * D:].astype(F32) + b_f.astype(F32))
    if past is None:
        k_all, v_all, logf_all = k, v, logf
    else:
        k_all = jnp.concatenate([past[0].astype(k.dtype), k], axis=1)
        v_all = jnp.concatenate([past[1].astype(v.dtype), v], axis=1)
        logf_all = jnp.concatenate([past[2].astype(F32), logf], axis=1)
    Lk = k_all.shape[1]
    k_pos = jnp.arange(Lk)
    c = jnp.cumsum(logf_all, axis=1)
    ck = jnp.transpose(c, (0, 2, 1))
    cq = c[:, Lk - L:]
    scale = dh ** -0.5

    def block(qb, cqb, qpos):
        s = jnp.einsum('bqhd,bkhd->bhqk', qb, k_all).astype(F32) * scale
        s = s + jnp.transpose(cqb, (0, 2, 1))[..., None] - ck[:, :, None, :]
        s = jnp.where(k_pos[None, :] <= qpos[:, None], s, -jnp.inf)
        p = jax.nn.softmax(s, axis=-1)
        return jnp.einsum('bhqk,bkhd->bqhd', p.astype(v_all.dtype), v_all)

    o = sweep_query_blocks(block, (q, cq), pos)
    y = o.reshape(B, L, D) @ w_o
    return y, k, v, logf


def pool_mixer(x, start, prev, w_pool, pool_scale):
    B, L, D = x.shape
    G = D // POOL_GROUPS
    xc = jnp.concatenate([prev.astype(x.dtype), x], axis=1)
    xf = xc.astype(F32)
    cs = jnp.concatenate([jnp.zeros((B, 1, D), F32), jnp.cumsum(xf, axis=1)], axis=1)
    pos = start + jnp.arange(L)
    r = POOL_KEEP + 1
    means = []
    for gi, w in enumerate(POOL_WINDOWS):
        sl = slice(gi * G, (gi + 1) * G)
        win = cs[:, r:r + L, sl] - cs[:, r - w:r - w + L, sl]
        cnt = jnp.minimum(pos + 1, w).astype(F32)[None, :, None]
        means.append(win / cnt)
    mixed = jnp.concatenate(means, axis=-1) - xf[:, POOL_KEEP:]
    y = jnp.einsum('blgc,gce->blge', mixed.reshape(B, L, POOL_GROUPS, G).astype(x.dtype), w_pool)
    return y.reshape(B, L, D) * pool_scale, xc[:, -POOL_KEEP:]


def gdn_chunked(q, k, v, beta, g, S0):
    B, L, H, dk = q.shape
    dv = v.shape[-1]
    C = min(GDN_CHUNK, L)
    pad = (-L) % C
    n = (L + pad) // C

    def prep(t):
        t = jnp.pad(t, [(0, 0), (0, pad)] + [(0, 0)] * (t.ndim - 2))
        t = jnp.moveaxis(t, 2, 1)
        return t.reshape(B, H, n, C, *t.shape[3:])

    q, k, v, beta, g = prep(q), prep(k), prep(v), prep(beta), prep(g)
    gc = jnp.cumsum(g, axis=-1)
    idx = jnp.arange(C)
    incl = idx[:, None] >= idx[None, :]
    strict = idx[:, None] > idx[None, :]
    decay = jnp.exp(jnp.where(incl, gc[..., :, None] - gc[..., None, :], -jnp.inf))
    kk = jnp.einsum('bhnid,bhnjd->bhnij', k, k)
    tri = jnp.where(strict, beta[..., :, None] * kk * decay, 0.0) + jnp.eye(C, dtype=q.dtype)
    rhs = jnp.concatenate([beta[..., None] * v, (beta * jnp.exp(gc))[..., None] * k], axis=-1)
    sol = lax.linalg.triangular_solve(tri, rhs, left_side=True, lower=True, unit_diagonal=True)
    u, wk = sol[..., :dv], sol[..., dv:]
    aqk = jnp.einsum('bhnid,bhnjd->bhnij', q, k) * decay
    qg = q * jnp.exp(gc)[..., None]
    kd = k * jnp.exp(gc[..., -1:] - gc)[..., None]
    g_end = jnp.exp(gc[..., -1])
    xs = tuple(jnp.moveaxis(t, 2, 0) for t in (u, wk, aqk, qg, kd, g_end))

    def step(S, c):
        u_c, wk_c, aqk_c, qg_c, kd_c, g_c = c
        w = u_c - jnp.einsum('bhck,bhkv->bhcv', wk_c, S)
        o = jnp.einsum('bhck,bhkv->bhcv', qg_c, S) + jnp.einsum('bhij,bhjv->bhiv', aqk_c, w)
        S = g_c[..., None, None] * S + jnp.einsum('bhck,bhcv->bhkv', kd_c, w)
        return S, o

    S, o = lax.scan(step, S0, xs)
    o = jnp.moveaxis(o, 0, 2).reshape(B, H, n * C, dv)[:, :, :L]
    return jnp.moveaxis(o, 1, 2), S


def gated_deltanet(x, S0, conv_prev, w_in, conv_w, A_log, dt_bias, norm_w, w_o):
    B, L, D = x.shape
    H, dk = GDN_HEADS, HEAD_DIM
    proj = x @ w_in
    qkv = proj[..., :3 * D]
    z = proj[..., 3 * D:4 * D]
    b = proj[..., 4 * D:4 * D + H]
    a = proj[..., 4 * D + H:]
    qkv_c = jnp.concatenate([conv_prev.astype(qkv.dtype), qkv], axis=1)
    conv = qkv_c[:, 0:L] * conv_w[0]
    for j in range(1, GDN_CONV):
        conv = conv + qkv_c[:, j:j + L] * conv_w[j]
    conv = jax.nn.silu(conv).astype(F32)
    q, k, v = jnp.split(conv, 3, axis=-1)
    q = l2norm(q.reshape(B, L, H, dk)) * (dk ** -0.5)
    k = l2norm(k.reshape(B, L, H, dk))
    v = v.reshape(B, L, H, dk)
    beta = jax.nn.sigmoid(b.astype(F32))
    g = -jnp.exp(A_log.astype(F32)) * jax.nn.softplus(a.astype(F32) + dt_bias.astype(F32))
    o, S = gdn_chunked(q, k, v, beta, g, S0.astype(F32))
    o = rms_norm(o, norm_w) * jax.nn.silu(z.reshape(B, L, H, dk).astype(F32))
    y = o.reshape(B, L, D).astype(x.dtype) @ w_o
    return y, S, qkv_c[:, -(GDN_CONV - 1):]


def channel_mixer(x, w_up, w_down):
    return jnp.square(jax.nn.relu(x @ w_up)) @ w_down


def per_layer_embed(x, p, w_gate, w_proj):
    return x + jax.nn.sigmoid(x @ w_gate) * (p.astype(x.dtype) @ w_proj)


def setup_inputs(seed: int = 0) -> dict:
    key = jax.random.key(seed)
    keys = iter(jax.random.split(key, 64))
    D = D_MODEL
    H_f, H_g = FOX_HEADS, GDN_HEADS
    G = D // POOL_GROUPS
    n_pages = PAST_LEN // PAGE_SIZE
    n_phys = (DEC_BATCH * n_pages * 5) // 4
    beta = (8 * DEPTH) ** -0.25
    s = D ** -0.5

    def nrm(shape, scale):
        return jax.random.normal(next(keys), shape, F32) * scale

    x_prompt = nrm((BATCH, SEQ, D), 1.0)
    x_sample = nrm((DEC_BATCH, DEC_SEQ, D), 1.0)
    cache_k_diff = nrm((n_phys, PAGE_SIZE, 2 * DIFF_HEADS, HEAD_DIM), 1.0)
    cache_v_diff = nrm((n_phys, PAGE_SIZE, DIFF_HEADS, 2 * HEAD_DIM), beta)
    cache_k_fox = nrm((n_phys, PAGE_SIZE, H_f, HEAD_DIM), 1.0)
    cache_v_fox = nrm((n_phys, PAGE_SIZE, H_f, HEAD_DIM), beta)
    b_f_fox = jnp.linspace(1.0, 7.0, H_f, dtype=F32) + nrm((H_f,), 0.1)
    cache_logf_fox = jax.nn.log_sigmoid(b_f_fox + nrm((n_phys, PAGE_SIZE, H_f), 0.5))
    state_pool = nrm((DEC_BATCH, POOL_KEEP, D), 1.0)
    state_gdn_conv = nrm((DEC_BATCH, GDN_CONV - 1, 3 * D), 1.0)
    state_gdn_S = nrm((DEC_BATCH, H_g, HEAD_DIM, HEAD_DIM), 0.1)
    page_table = jax.random.permutation(next(keys), n_phys)[:DEC_BATCH * n_pages].reshape(DEC_BATCH, n_pages).astype(jnp.int32)
    p_prompt = nrm((DEPTH, BATCH, SEQ, PLE_DIM), 1.0)
    p_sample = nrm((DEPTH, DEC_BATCH, DEC_SEQ, PLE_DIM), 1.0)

    ln1_g = 1.0 + nrm((DEPTH, D), 0.02)
    ln1_b = nrm((DEPTH, D), 0.02)
    ln2_g = 1.0 + nrm((DEPTH, D), 0.02)
    ln2_b = nrm((DEPTH, D), 0.02)
    w_mlp_up = nrm((DEPTH, D, D_FF), s)
    w_mlp_down = nrm((DEPTH, D_FF, D), D_FF ** -0.5 * beta)
    w_ple_gate = nrm((DEPTH, D, D), s)
    w_ple_proj = nrm((DEPTH, PLE_DIM, D), PLE_DIM ** -0.5)

    w_qkv_diff = jnp.concatenate([nrm((D, D), s), nrm((D, D), s), nrm((D, D), s * beta)], axis=1)
    lam_q1 = nrm((HEAD_DIM,), 0.1)
    lam_k1 = nrm((HEAD_DIM,), 0.1)
    lam_q2 = nrm((HEAD_DIM,), 0.1)
    lam_k2 = nrm((HEAD_DIM,), 0.1)
    subln_diff = 1.0 + nrm((2 * HEAD_DIM,), 0.02)
    w_o_diff = nrm((D, D), s * beta)

    w_in_fox = jnp.concatenate([nrm((D, D), s), nrm((D, D), s), nrm((D, D), s * beta), nrm((D, H_f), s)], axis=1)
    w_o_fox = nrm((D, D), s * beta)

    w_pool = nrm((POOL_GROUPS, G, G), G ** -0.5 * beta)
    pool_scale = 1.0 + nrm((D,), 0.1)

    w_in_gdn = jnp.concatenate([nrm((D, D), s), nrm((D, D), s), nrm((D, D), s * beta),
                                nrm((D, D), s), nrm((D, H_g), s), nrm((D, H_g), s)], axis=1)
    conv_gdn = nrm((GDN_CONV, 3 * D), GDN_CONV ** -0.5)
    A_log_gdn = jnp.log(jax.random.uniform(next(keys), (H_g,), F32, 1.0, 16.0))
    dt = jnp.exp(jax.random.uniform(next(keys), (H_g,), F32, math.log(1e-3), math.log(1e-1)))
    dt_bias_gdn = dt + jnp.log(-jnp.expm1(-dt))
    norm_gdn = 1.0 + nrm((HEAD_DIM,), 0.02)
    w_o_gdn = nrm((D, D), s * beta)

    return {
        'x_prompt': x_prompt, 'x_sample': x_sample,
        'cache_k_diff': cache_k_diff, 'cache_v_diff': cache_v_diff,
        'cache_k_fox': cache_k_fox, 'cache_v_fox': cache_v_fox, 'cache_logf_fox': cache_logf_fox,
        'state_pool': state_pool, 'state_gdn_conv': state_gdn_conv, 'state_gdn_S': state_gdn_S,
        'page_table': page_table, 'p_prompt': p_prompt, 'p_sample': p_sample,
        'ln1_g': ln1_g, 'ln1_b': ln1_b, 'ln2_g': ln2_g, 'ln2_b': ln2_b,
        'w_mlp_up': w_mlp_up, 'w_mlp_down': w_mlp_down, 'w_ple_gate': w_ple_gate, 'w_ple_proj': w_ple_proj,
        'w_qkv_diff': w_qkv_diff, 'lam_q1': lam_q1, 'lam_k1': lam_k1, 'lam_q2': lam_q2, 'lam_k2': lam_k2,
        'subln_diff': subln_diff, 'w_o_diff': w_o_diff,
        'w_in_fox': w_in_fox, 'b_f_fox': b_f_fox, 'w_o_fox': w_o_fox,
        'w_pool': w_pool, 'pool_scale': pool_scale,
        'w_in_gdn': w_in_gdn, 'conv_gdn': conv_gdn, 'A_log_gdn': A_log_gdn, 'dt_bias_gdn': dt_bias_gdn,
        'norm_gdn': norm_gdn, 'w_o_gdn': w_o_gdn,
    }


def reference(x_prompt, x_sample, cache_k_diff, cache_v_diff, cache_k_fox, cache_v_fox, cache_logf_fox,
              state_pool, state_gdn_conv, state_gdn_S, page_table, p_prompt, p_sample,
              ln1_g, ln1_b, ln2_g, ln2_b, w_mlp_up, w_mlp_down, w_ple_gate, w_ple_proj,
              w_qkv_diff, lam_q1, lam_k1, lam_q2, lam_k2, subln_diff, w_o_diff,
              w_in_fox, b_f_fox, w_o_fox, w_pool, pool_scale,
              w_in_gdn, conv_gdn, A_log_gdn, dt_bias_gdn, norm_gdn, w_o_gdn):
    alpha = (2 * DEPTH) ** 0.25
    past_len = page_table.shape[1] * cache_k_diff.shape[1]
    B = x_prompt.shape[0]
    D = x_prompt.shape[-1]
    xp, xs = x_prompt, x_sample
    for i in range(DEPTH):
        kind = i % N_MIXERS
        if kind == 0:
            lambda_init = 0.8 - 0.6 * math.exp(-0.3 * i)
            mp, k_diff_p, v_diff_p = diff_attention(xp, 0, None, w_qkv_diff, lam_q1, lam_k1, lam_q2, lam_k2,
                                                    subln_diff, w_o_diff, lambda_init)
            past = (gather_pages(cache_k_diff, page_table), gather_pages(cache_v_diff, page_table))
            ms, k_diff_s, v_diff_s = diff_attention(xs, past_len, past, w_qkv_diff, lam_q1, lam_k1, lam_q2, lam_k2,
                                                    subln_diff, w_o_diff, lambda_init)
        elif kind == 1:
            mp, k_fox_p, v_fox_p, logf_fox_p = fox_attention(xp, 0, None, w_in_fox, b_f_fox, w_o_fox)
            past = (gather_pages(cache_k_fox, page_table), gather_pages(cache_v_fox, page_table),
                    gather_pages(cache_logf_fox, page_table))
            ms, k_fox_s, v_fox_s, logf_fox_s = fox_attention(xs, past_len, past, w_in_fox, b_f_fox, w_o_fox)
        elif kind == 2:
            mp, pool_p = pool_mixer(xp, 0, jnp.zeros((B, POOL_KEEP, D), xp.dtype), w_pool, pool_scale)
            ms, pool_s = pool_mixer(xs, past_len, state_pool, w_pool, pool_scale)
        else:
            S0 = jnp.zeros((B, GDN_HEADS, HEAD_DIM, HEAD_DIM), F32)
            c0 = jnp.zeros((B, GDN_CONV - 1, 3 * D), xp.dtype)
            mp, S_p, conv_p = gated_deltanet(xp, S0, c0, w_in_gdn, conv_gdn, A_log_gdn, dt_bias_gdn, norm_gdn, w_o_gdn)
            ms, S_s, conv_s = gated_deltanet(xs, state_gdn_S, state_gdn_conv, w_in_gdn, conv_gdn, A_log_gdn,
                                             dt_bias_gdn, norm_gdn, w_o_gdn)
        xp = layer_norm(alpha * xp + mp, ln1_g[i], ln1_b[i])
        xs = layer_norm(alpha * xs + ms, ln1_g[i], ln1_b[i])
        xp = layer_norm(alpha * xp + channel_mixer(xp, w_mlp_up[i], w_mlp_down[i]), ln2_g[i], ln2_b[i])
        xs = layer_norm(alpha * xs + channel_mixer(xs, w_mlp_up[i], w_mlp_down[i]), ln2_g[i], ln2_b[i])
        xp = per_layer_embed(xp, p_prompt[i], w_ple_gate[i], w_ple_proj[i])
        xs = per_layer_embed(xs, p_sample[i], w_ple_gate[i], w_ple_proj[i])
    return (xp, xs, k_diff_p, v_diff_p, k_diff_s, v_diff_s,
            k_fox_p, v_fox_p, logf_fox_p, k_fox_s, v_fox_s, logf_fox_s,
            pool_p, pool_s, conv_p, S_p, conv_s, S_s)
```

```python
import functools
import math

import jax
import jax.numpy as jnp
from jax import lax
from jax.experimental import pallas as pl
from jax.experimental.pallas import tpu as pltpu

F32 = jnp.float32
BF16 = jnp.bfloat16

HEAD_DIM = 128
POOL_WINDOWS = (2, 4, 8, 16)
POOL_HALO = 16
CONV_HALO = 8
GDN_CHUNK = 64
GDN_HEAD_GROUP = 8
ROPE_THETA = 10000.0
LN_EPS = 1e-5
RMS_EPS = 1e-5
N_MIXERS = 4
LANES = 128
VMEM_LIMIT = 56 * 1024 * 1024

_NT = (((1,), (1,)), ((), ()))
_TN = (((0,), (0,)), ((), ()))


def _params(*sem):
    return pltpu.CompilerParams(dimension_semantics=sem, vmem_limit_bytes=VMEM_LIMIT)


def _pick(dim, target, align):
    best = None
    for t in range(align, min(dim, target) + 1, align):
        if dim % t == 0:
            best = t
    return best if best is not None else dim


def _sigmoid(x):
    return 1.0 / (1.0 + jnp.exp(-x))


def _softplus(x):
    return jnp.maximum(x, 0.0) + jnp.log1p(jnp.exp(-jnp.abs(x)))


def _split3(a):
    hi = a.astype(BF16)
    r = a - hi.astype(F32)
    mid = r.astype(BF16)
    lo = (r - mid.astype(F32)).astype(BF16)
    return hi, mid, lo


def _split2(a):
    hi = a.astype(BF16)
    return hi, (a - hi.astype(F32)).astype(BF16)


def _dot(a, b):
    return jnp.dot(a, b, preferred_element_type=F32)


def _dot_hi(a, b):
    a1, a2 = _split2(a)
    b1, b2 = _split2(b)
    return _dot(a1, b1) + (_dot(a1, b2) + _dot(a2, b1))


def _tri_dot_hi(tri_bf16, x):
    x1, x2, x3 = _split3(x)
    return _dot(tri_bf16, x1) + _dot(tri_bf16, x2) + _dot(tri_bf16, x3)


def _mm_body(*refs, nk, n_extra, n_out, epilogue):
    x_ref, w_ref = refs[0], refs[1]
    extra = refs[2:2 + n_extra]
    outs = refs[2 + n_extra:2 + n_extra + n_out]
    acc_ref = refs[-1]
    k = pl.program_id(2)
    part = _dot(x_ref[...].astype(BF16), w_ref[...].astype(BF16))

    @pl.when(k == 0)
    def _():
        acc_ref[...] = part

    @pl.when(k > 0)
    def _():
        acc_ref[...] += part

    @pl.when(k == nk - 1)
    def _():
        epilogue(acc_ref[...], extra, outs)


def _matmul(x, w, *, epilogue, out_dtypes, col_off=0, n_cols=None, extras=(),
            tm_target=1376, tn_target=1024, tk_target=1024):
    M, K = x.shape
    n_cols = w.shape[1] - col_off if n_cols is None else n_cols
    tm = _pick(M, tm_target, 16)
    tn = _pick(n_cols, tn_target, LANES)
    tk = _pick(K, tk_target, LANES)
    assert col_off % tn == 0
    joff = col_off // tn
    nk = K // tk
    grid = (M // tm, n_cols // tn, nk)
    in_specs = [pl.BlockSpec((tm, tk), lambda i, j, k: (i, k)),
                pl.BlockSpec((tk, tn), lambda i, j, k: (k, j + joff))]
    args = [x, w]
    for arr, bshape, imap in extras:
        in_specs.append(pl.BlockSpec(bshape, imap))
        args.append(arr)
    out_shape = [jax.ShapeDtypeStruct((M, n_cols), dt) for dt in out_dtypes]
    out_specs = [pl.BlockSpec((tm, tn), lambda i, j, k: (i, j)) for _ in out_dtypes]
    body = functools.partial(_mm_body, nk=nk, n_extra=len(extras), n_out=len(out_dtypes),
                             epilogue=epilogue)
    res = pl.pallas_call(
        body, grid=grid, in_specs=in_specs, out_specs=out_specs, out_shape=out_shape,
        scratch_shapes=[pltpu.VMEM((tm, tn), F32)],
        compiler_params=_params("parallel", "parallel", "arbitrary"),
    )(*args)
    return res, (tm, tn)


def _epi_store(acc, extra, outs):
    for o in outs:
        o[...] = acc.astype(o.dtype)


def _epi_rope(acc, extra, outs, *, scale):
    cos = extra[0][...]
    sin = extra[1][...]
    parts = []
    for c in range(acc.shape[1] // HEAD_DIM):
        xc = acc[:, c * HEAD_DIM:(c + 1) * HEAD_DIM]
        parts.append(xc * cos + pltpu.roll(xc, HEAD_DIM // 2, 1) * sin)
    r = jnp.concatenate(parts, axis=1)
    if scale != 1.0:
        r = r * scale
    for o in outs:
        o[...] = r.astype(o.dtype)


def _epi_scale(acc, extra, outs, *, scale):
    for o in outs:
        o[...] = (acc * scale).astype(o.dtype)


def _epi_relu2(acc, extra, outs):
    h = jnp.maximum(acc, 0.0)
    outs[0][...] = (h * h).astype(outs[0].dtype)


def _epi_residual(acc, extra, outs, *, alpha):
    outs[0][...] = alpha * extra[0][...] + acc


def _epi_ple(acc, extra, outs):
    x_ref, p_ref, wp_ref = extra
    pp = _dot(p_ref[...].astype(BF16), wp_ref[...].astype(BF16))
    y = x_ref[...] + _sigmoid(acc) * pp
    for o in outs:
        o[...] = y.astype(o.dtype)


def _epi_logsigmoid(acc, extra, outs):
    z = acc + extra[0][...]
    outs[0][...] = -(jnp.maximum(-z, 0.0) + jnp.log1p(jnp.exp(-jnp.abs(z))))


def _tile_extra(arr, tm, tn):
    return (arr, (tm, tn), lambda i, j, k: (i, j))


def _mm_plain(x, w, out_dtypes, **kw):
    res, _ = _matmul(x, w, epilogue=_epi_store, out_dtypes=out_dtypes, **kw)
    return res


def _mm_residual(x, w, resid, alpha):
    M, N = resid.shape
    tm, tn = _pick(M, 1376, 16), _pick(N, 1024, LANES)
    res, _ = _matmul(x, w, epilogue=functools.partial(_epi_residual, alpha=alpha),
                     out_dtypes=[F32], extras=[_tile_extra(resid, tm, tn)])
    return res[0]


def _ln_body(y_ref, g_ref, b_ref, o32_ref, o16_ref):
    y = y_ref[...]
    mu = jnp.mean(y, axis=-1, keepdims=True)
    d = y - mu
    var = jnp.mean(d * d, axis=-1, keepdims=True)
    out = d * lax.rsqrt(var + LN_EPS) * g_ref[...] + b_ref[...]
    o32_ref[...] = out
    o16_ref[...] = out.astype(BF16)


def _layer_norm(y, g, b):
    M, D = y.shape
    tm = _pick(M, 256, 16)
    row = pl.BlockSpec((tm, D), lambda i: (i, 0))
    vec = pl.BlockSpec((1, D), lambda i: (0, 0))
    return pl.pallas_call(
        _ln_body, grid=(M // tm,), in_specs=[row, vec, vec], out_specs=[row, row],
        out_shape=[jax.ShapeDtypeStruct((M, D), F32), jax.ShapeDtypeStruct((M, D), BF16)],
        compiler_params=_params("parallel"),
    )(y, g.reshape(1, D), b.reshape(1, D))


def _lambda_full(lam_ref, lambda_init):
    lam = lam_ref[...]
    e1 = jnp.exp(jnp.sum(lam[0:1] * lam[1:2], axis=-1, keepdims=True))
    e2 = jnp.exp(jnp.sum(lam[2:3] * lam[3:4], axis=-1, keepdims=True))
    return e1 - e2 + lambda_init


def _diff_finish(o0, o1, lam, subw, lambda_init):
    o = o0 - lam * o1
    o = o * lax.rsqrt(jnp.mean(o * o, axis=-1, keepdims=True) + RMS_EPS) * subw
    return o * (1.0 - lambda_init)


def _flash_body(*refs, mode, nk, tq, tk, lambda_init):
    if mode == "diff":
        q_ref, k_ref, v_ref, lam_ref, subw_ref, o_ref, m_scr, l_scr, acc_scr = refs
    else:
        q_ref, k_ref, v_ref, cq_ref, ck_ref, o_ref, m_scr, l_scr, acc_scr = refs
    qi = pl.program_id(2)
    ki = pl.program_id(3)
    dv = acc_scr.shape[-1]

    @pl.when(ki == 0)
    def _():
        m_scr[...] = jnp.full(m_scr.shape, -jnp.inf, F32)
        l_scr[...] = jnp.zeros(l_scr.shape, F32)
        acc_scr[...] = jnp.zeros(acc_scr.shape, F32)

    @pl.when(ki <= qi)
    def _():
        q = q_ref[...]
        k = k_ref[...]
        v = v_ref[...]
        row = qi * tq + lax.broadcasted_iota(jnp.int32, (tq, tk), 0)
        col = ki * tk + lax.broadcasted_iota(jnp.int32, (tq, tk), 1)
        mask = col <= row
        for c in range(2):
            sl = slice(c * HEAD_DIM, (c + 1) * HEAD_DIM)
            s = lax.dot_general(q[:, sl], k[:, sl], _NT, preferred_element_type=F32)
            if mode == "fox":
                s = s + (cq_ref[0][:, c:c + 1] - ck_ref[0][c:c + 1, :])
            s = jnp.where(mask, s, -jnp.inf)
            m_old = m_scr[c]
            m_new = jnp.maximum(m_old, jnp.max(s, axis=-1, keepdims=True))
            alpha = jnp.exp(m_old - m_new)
            p = jnp.exp(s - m_new)
            l_scr[c] = alpha * l_scr[c] + jnp.sum(p, axis=-1, keepdims=True)
            vv = v if mode == "diff" else v[:, sl]
            acc_scr[c] = alpha * acc_scr[c] + _dot(p.astype(BF16), vv)
            m_scr[c] = m_new

    @pl.when(ki == nk - 1)
    def _():
        o0 = acc_scr[0] / l_scr[0]
        o1 = acc_scr[1] / l_scr[1]
        if mode == "diff":
            lam = _lambda_full(lam_ref, lambda_init)
            o = _diff_finish(o0, o1, lam, subw_ref[...], lambda_init)
        else:
            o = jnp.concatenate([o0, o1], axis=1)
        o_ref[...] = o.astype(o_ref.dtype)


def _flash_prompt(q, k, v, *, B, L, mode, extra, lambda_init=0.0):
    D = q.shape[1]
    W = 2 * HEAD_DIM
    tq = tk = _pick(L, 512, LANES)
    nq = L // tq
    grid = (B, D // W, nq, nq)
    qspec = pl.BlockSpec((tq, W), lambda b, h, qi, ki: (b * nq + qi, h))
    kspec = pl.BlockSpec((tk, W), lambda b, h, qi, ki: (b * nq + jnp.minimum(ki, qi), h))
    if mode == "diff":
        lam4, subw = extra
        especs = [pl.BlockSpec(lam4.shape, lambda b, h, qi, ki: (0, 0)),
                  pl.BlockSpec(subw.shape, lambda b, h, qi, ki: (0, 0))]
        dv = W
    else:
        cqh, ckh = extra
        especs = [pl.BlockSpec((1, tq, 2), lambda b, h, qi, ki: (h, b * nq + qi, 0)),
                  pl.BlockSpec((1, 2, tk), lambda b, h, qi, ki: (h, 0, b * nq + jnp.minimum(ki, qi)))]
        dv = HEAD_DIM
    body = functools.partial(_flash_body, mode=mode, nk=nq, tq=tq, tk=tk, lambda_init=lambda_init)
    return pl.pallas_call(
        body, grid=grid, in_specs=[qspec, kspec, kspec] + especs,
        out_specs=pl.BlockSpec((tq, W), lambda b, h, qi, ki: (b * nq + qi, h)),
        out_shape=jax.ShapeDtypeStruct((B * L, D), BF16),
        scratch_shapes=[pltpu.VMEM((2, tq, 1), F32), pltpu.VMEM((2, tq, 1), F32),
                        pltpu.VMEM((2, tq, dv), F32)],
        compiler_params=_params("parallel", "parallel", "parallel", "arbitrary"),
    )(q, k, v, *extra)


def _decode_body(*refs, mode, npages, nh, ls, lambda_init):
    pt_ref = refs[0]
    if mode == "diff":
        (q_ref, kc_ref, vc_ref, kn_ref, vn_ref, lam_ref, subw_ref,
         o_ref, m_scr, l_scr, acc_scr) = refs[1:]
    else:
        (q_ref, kc_ref, vc_ref, kn_ref, vn_ref, cq_ref, ckc_ref, ckn_ref,
         o_ref, m_scr, l_scr, acc_scr) = refs[1:]
    del pt_ref
    p = pl.program_id(1)
    P = kc_ref.shape[1]

    @pl.when(p == 0)
    def _():
        m_scr[...] = jnp.full(m_scr.shape, -jnp.inf, F32)
        l_scr[...] = jnp.zeros(l_scr.shape, F32)
        acc_scr[...] = jnp.zeros(acc_scr.shape, F32)

    def step(k_all, v_all, ck, mask):
        q = q_ref[0]
        probs = []
        for h in range(nh):
            sl = slice(h * HEAD_DIM, (h + 1) * HEAD_DIM)
            s = lax.dot_general(q[:, sl], k_all[:, sl], _NT, preferred_element_type=F32)
            if mode == "fox":
                s = s + (cq_ref[0][:, h:h + 1] - ck[h:h + 1, :])
            if mask is not None:
                s = jnp.where(mask, s, -jnp.inf)
            m_old = m_scr[h]
            m_new = jnp.maximum(m_old, jnp.max(s, axis=-1, keepdims=True))
            alpha = jnp.exp(m_old - m_new)
            pr = jnp.exp(s - m_new)
            l_scr[h] = alpha * l_scr[h] + jnp.sum(pr, axis=-1, keepdims=True)
            m_scr[h] = m_new
            probs.append((alpha, pr.astype(BF16)))
        if mode == "diff":
            for hh in range(nh // 2):
                (a0, p0), (a1, p1) = probs[2 * hh], probs[2 * hh + 1]
                vv = v_all[:, hh * 2 * HEAD_DIM:(hh + 1) * 2 * HEAD_DIM]
                pv = _dot(jnp.concatenate([p0, p1], axis=0), vv)
                al = jnp.concatenate([a0, a1], axis=0)
                acc_scr[hh] = al * acc_scr[hh] + pv
        else:
            for h in range(nh):
                a0, p0 = probs[h]
                acc_scr[h] = a0 * acc_scr[h] + _dot(p0, v_all[:, h * HEAD_DIM:(h + 1) * HEAD_DIM])

    @pl.when(p < npages)
    def _():
        ck = ckc_ref[0] if mode == "fox" else None
        step(kc_ref[0].astype(BF16), vc_ref[0].astype(BF16), ck, None)

    @pl.when(p == npages)
    def _():
        row = lax.broadcasted_iota(jnp.int32, (ls, P), 0)
        col = lax.broadcasted_iota(jnp.int32, (ls, P), 1)
        ck = ckn_ref[0] if mode == "fox" else None
        step(kn_ref[0], vn_ref[0], ck, col <= row)
        outs = []
        if mode == "diff":
            lam = _lambda_full(lam_ref, lambda_init)
            for hh in range(nh // 2):
                acc = acc_scr[hh]
                o0 = acc[:ls] / l_scr[2 * hh]
                o1 = acc[ls:] / l_scr[2 * hh + 1]
                outs.append(_diff_finish(o0, o1, lam, subw_ref[...], lambda_init))
        else:
            for h in range(nh):
                outs.append(acc_scr[h] / l_scr[h])
        o_ref[0] = jnp.concatenate(outs, axis=1).astype(o_ref.dtype)


def _decode_attention(q_s, k_new, v_new, cache_k, cache_v, page_table, *, mode, extra,
                      lambda_init=0.0):
    Bs, Ls, D = q_s.shape
    P = cache_k.shape[1]
    npages = page_table.shape[1]
    nh = D // HEAD_DIM
    pad = ((0, 0), (0, P - Ls), (0, 0))
    kn = jnp.pad(k_new, pad)
    vn = jnp.pad(v_new, pad)
    page = lambda b, p, pt: (pt[b, jnp.minimum(p, npages - 1)], 0, 0)
    per_b = lambda b, p, pt: (b, 0, 0)
    in_specs = [pl.BlockSpec((1, Ls, D), per_b),
                pl.BlockSpec((1, P, D), page), pl.BlockSpec((1, P, D), page),
                pl.BlockSpec((1, P, D), per_b), pl.BlockSpec((1, P, D), per_b)]
    if mode == "diff":
        lam4, subw = extra
        in_specs += [pl.BlockSpec(lam4.shape, lambda b, p, pt: (0, 0)),
                     pl.BlockSpec(subw.shape, lambda b, p, pt: (0, 0))]
        acc_shape = (nh // 2, 2 * Ls, 2 * HEAD_DIM)
    else:
        cq, ckc, ckn = extra
        in_specs += [pl.BlockSpec((1, Ls, nh), per_b),
                     pl.BlockSpec((1, nh, P), lambda b, p, pt: (b, 0, jnp.minimum(p, npages - 1))),
                     pl.BlockSpec((1, nh, P), per_b)]
        acc_shape = (nh, Ls, HEAD_DIM)
    body = functools.partial(_decode_body, mode=mode, npages=npages, nh=nh, ls=Ls,
                             lambda_init=lambda_init)
    grid_spec = pltpu.PrefetchScalarGridSpec(
        num_scalar_prefetch=1, grid=(Bs, npages + 1), in_specs=in_specs,
        out_specs=pl.BlockSpec((1, Ls, D), per_b),
        scratch_shapes=[pltpu.VMEM((nh, Ls, 1), F32), pltpu.VMEM((nh, Ls, 1), F32),
                        pltpu.VMEM(acc_shape, F32)])
    return pl.pallas_call(
        body, grid_spec=grid_spec, out_shape=jax.ShapeDtypeStruct((Bs, Ls, D), BF16),
        compiler_params=_params("parallel", "arbitrary"),
    )(page_table, q_s, cache_k, cache_v, kn, vn, *extra)


def _cumsum_body(tbl_ref, x_ref, init_ref, o_ref, carry, *, tb):
    del tbl_ref
    j = pl.program_id(1)

    @pl.when(j == 0)
    def _():
        carry[...] = init_ref[0]

    r = lax.broadcasted_iota(jnp.int32, (tb, tb), 0)
    c = lax.broadcasted_iota(jnp.int32, (tb, tb), 1)
    tri = (r >= c).astype(BF16)
    out = _tri_dot_hi(tri, x_ref[0]) + carry[...]
    o_ref[0] = out
    carry[...] = out[tb - 1:tb, :]


def _cumsum_blocks(src, table, init):
    _, tb, H = src.shape
    S, nb = table.shape
    grid_spec = pltpu.PrefetchScalarGridSpec(
        num_scalar_prefetch=1, grid=(S, nb),
        in_specs=[pl.BlockSpec((1, tb, H), lambda s, j, t: (t[s, j], 0, 0)),
                  pl.BlockSpec((1, 1, H), lambda s, j, t: (s, 0, 0))],
        out_specs=pl.BlockSpec((1, tb, H), lambda s, j, t: (s, j, 0)),
        scratch_shapes=[pltpu.VMEM((1, H), F32)])
    return pl.pallas_call(
        functools.partial(_cumsum_body, tb=tb), grid_spec=grid_spec,
        out_shape=jax.ShapeDtypeStruct((S, nb * tb, H), F32),
        compiler_params=_params("parallel", "arbitrary"),
    )(table, src, init)


def _pool_body(x_ref, prev_ref, w_ref, scale_ref, o_ref, carry, *, tl, start, alpha):
    g = pl.program_id(1)
    t = pl.program_id(2)

    @pl.when(t == 0)
    def _():
        carry[...] = prev_ref[0]

    x = x_ref[...]
    xe = jnp.concatenate([carry[...], x], axis=0)
    s2 = xe + pltpu.roll(xe, 1, 0)
    s4 = s2 + pltpu.roll(s2, 2, 0)
    s8 = s4 + pltpu.roll(s4, 4, 0)
    s16 = s8 + pltpu.roll(s8, 8, 0)
    win = jnp.where(g == 0, s2, jnp.where(g == 1, s4, jnp.where(g == 2, s8, s16)))[POOL_HALO:]
    width = jnp.left_shift(2, g)
    pos = start + t * tl + lax.broadcasted_iota(jnp.int32, (tl, 1), 0)
    cnt = jnp.minimum(pos + 1, width).astype(F32)
    mixed = win / cnt - x
    y = _dot(mixed.astype(BF16), w_ref[0].astype(BF16)) * scale_ref[...]
    o_ref[...] = alpha * x + y
    carry[...] = xe[tl:tl + POOL_HALO]


def _pool_mixer(x, prev, w_pool, pool_scale, *, row0, B, L, start, alpha):
    D = x.shape[1]
    ng = len(POOL_WINDOWS)
    G = D // ng
    tl = _pick(L, 512, 8)
    nt = L // tl
    r0 = row0 // tl
    assert row0 % tl == 0
    body = functools.partial(_pool_body, tl=tl, start=start, alpha=alpha)
    return pl.pallas_call(
        body, grid=(B, ng, nt),
        in_specs=[pl.BlockSpec((tl, G), lambda b, g, t: (r0 + b * nt + t, g)),
                  pl.BlockSpec((1, POOL_HALO, G), lambda b, g, t: (b, 0, g)),
                  pl.BlockSpec((1, G, G), lambda b, g, t: (g, 0, 0)),
                  pl.BlockSpec((1, G), lambda b, g, t: (0, g))],
        out_specs=pl.BlockSpec((tl, G), lambda b, g, t: (b * nt + t, g)),
        out_shape=jax.ShapeDtypeStruct((B * L, D), F32),
        scratch_shapes=[pltpu.VMEM((POOL_HALO, G), F32)],
        compiler_params=_params("parallel", "parallel", "arbitrary"),
    )(x, prev, w_pool, pool_scale.reshape(1, D))


def _conv_body(x_ref, prev_ref, w_ref, o_ref, carry, *, tl, ncb_part, qscale):
    cb = pl.program_id(1)
    t = pl.program_id(2)

    @pl.when(t == 0)
    def _():
        carry[...] = prev_ref[0]

    x = x_ref[...]
    w = w_ref[...]
    xe = jnp.concatenate([carry[...], x], axis=0)
    x1 = pltpu.roll(xe, 1, 0)[CONV_HALO:]
    x2 = pltpu.roll(xe, 2, 0)[CONV_HALO:]
    x3 = pltpu.roll(xe, 3, 0)[CONV_HALO:]
    conv = x3 * w[0:1] + x2 * w[1:2] + x1 * w[2:3] + x * w[3:4]
    act = conv * _sigmoid(conv)
    part = cb // ncb_part
    scale = jnp.where(part == 0, qscale, 1.0)
    pieces = []
    for c in range(act.shape[1] // HEAD_DIM):
        a = act[:, c * HEAD_DIM:(c + 1) * HEAD_DIM]
        n = a * lax.rsqrt(jnp.sum(a * a, axis=-1, keepdims=True) + 1e-6) * scale
        pieces.append(jnp.where(part < 2, n, a))
    o_ref[...] = jnp.concatenate(pieces, axis=1)
    carry[...] = xe[tl:tl + CONV_HALO]


def _gdn_conv(qkv, prev, conv_w, *, row0, B, L, D):
    assert conv_w.shape[0] == 4
    C3 = qkv.shape[1]
    tc = _pick(D, 512, LANES)
    tl = _pick(L, 512, 8)
    nt = L // tl
    r0 = row0 // tl
    assert row0 % tl == 0
    body = functools.partial(_conv_body, tl=tl, ncb_part=D // tc, qscale=HEAD_DIM ** -0.5)
    return pl.pallas_call(
        body, grid=(B, C3 // tc, nt),
        in_specs=[pl.BlockSpec((tl, tc), lambda b, c, t: (r0 + b * nt + t, c)),
                  pl.BlockSpec((1, CONV_HALO, tc), lambda b, c, t: (b, 0, c)),
                  pl.BlockSpec((4, tc), lambda b, c, t: (0, c))],
        out_specs=pl.BlockSpec((tl, tc), lambda b, c, t: (b * nt + t, c)),
        out_shape=jax.ShapeDtypeStruct((B * L, C3), F32),
        scratch_shapes=[pltpu.VMEM((CONV_HALO, tc), F32)],
        compiler_params=_params("parallel", "parallel", "arbitrary"),
    )(qkv, prev, conv_w)


def _gdn_body(q_ref, k_ref, v_ref, z_ref, ab_ref, alog_ref, dtb_ref, nw_ref, s0_ref,
              o_ref, s_ref, gct_scr, *, C, HG, nh, valid_len):
    hg = pl.program_id(1)
    c = pl.program_id(2)

    @pl.when(c == 0)
    def _():
        s_ref[...] = s0_ref[...]

    ab = ab_ref[...]
    row = c * C + lax.broadcasted_iota(jnp.int32, (C, 1), 0)
    valid = row < valid_len
    beta = jnp.where(valid, _sigmoid(ab[:, :nh]), 0.0)
    g = jnp.where(valid, -jnp.exp(alog_ref[...]) * _softplus(ab[:, nh:] + dtb_ref[...]), 0.0)
    ii = lax.broadcasted_iota(jnp.int32, (C, C), 0)
    jj = lax.broadcasted_iota(jnp.int32, (C, C), 1)
    incl = ii >= jj
    strict = ii > jj
    gc = _tri_dot_hi(incl.astype(BF16), g)
    g_last = gc[C - 1:C, :]
    e_gc = jnp.exp(gc)
    e_kd = jnp.exp(g_last - gc)
    e_end = jnp.exp(g_last)
    gpad = jnp.concatenate([gc, jnp.zeros((C, LANES - nh), F32)], axis=1)
    gpad = jnp.concatenate([gpad, jnp.zeros((LANES - C, LANES), F32)], axis=0)
    gct_scr[...] = gpad.T
    lane_h = lax.broadcasted_iota(jnp.int32, (1, nh), 1)
    eye = (ii == jj).astype(F32)
    nw = nw_ref[...]
    rounds = max(int(math.ceil(math.log2(C))) - 1, 0)

    for j in range(HG):
        head = hg * HG + j
        sel = lane_h == head
        col = lambda a: jnp.sum(jnp.where(sel, a, 0.0), axis=1, keepdims=True)
        gcol, bcol, egc, ekd, eend = col(gc), col(beta), col(e_gc), col(e_kd), col(e_end)
        grow = gct_scr[pl.ds(head, 1), :][:, :C]
        sl = slice(j * HEAD_DIM, (j + 1) * HEAD_DIM)
        q = q_ref[:, sl]
        k = k_ref[:, sl]
        v = v_ref[:, sl]
        kb = k.astype(BF16)
        decay = jnp.exp(jnp.where(incl, gcol - grow, -jnp.inf))
        kk = lax.dot_general(kb, kb, _NT, preferred_element_type=F32)
        aqk = lax.dot_general(q.astype(BF16), kb, _NT, preferred_element_type=F32) * decay
        A = jnp.where(strict, bcol * kk * decay, 0.0)
        X = eye - A
        Pw = A
        for _ in range(rounds):
            Pw = _dot_hi(Pw, Pw)
            X = X + _dot_hi(X, Pw)
        rhs = jnp.concatenate([bcol * v, (bcol * egc) * k], axis=1)
        sol = _dot_hi(X, rhs)
        u = sol[:, :HEAD_DIM]
        wk = sol[:, HEAD_DIM:]
        S = s_ref[0, j]
        Sb = S.astype(BF16)
        w = u - _dot(wk.astype(BF16), Sb)
        wb = w.astype(BF16)
        o = _dot((q * egc).astype(BF16), Sb) + _dot(aqk.astype(BF16), wb)
        s_ref[0, j] = eend * S + lax.dot_general((k * ekd).astype(BF16), wb, _TN,
                                                 preferred_element_type=F32)
        zz = z_ref[:, sl]
        on = o * lax.rsqrt(jnp.mean(o * o, axis=-1, keepdims=True) + RMS_EPS) * nw
        o_ref[:, sl] = (on * (zz * _sigmoid(zz))).astype(o_ref.dtype)


def _gdn_core(qkvn, z, ab, S0, A_log, dt_bias, norm_w, *, B, L, valid_len, zrow0):
    D = qkvn.shape[1] // 3
    nh = D // HEAD_DIM
    C = GDN_CHUNK
    HG = min(GDN_HEAD_GROUP, nh)
    W = HG * HEAD_DIM
    nc = L // C
    npart = D // W
    zr0 = zrow0 // C
    assert L % C == 0 and zrow0 % C == 0
    body = functools.partial(_gdn_body, C=C, HG=HG, nh=nh, valid_len=valid_len)
    vec = lambda n: pl.BlockSpec((1, n), lambda b, h, c: (0, 0))
    return pl.pallas_call(
        body, grid=(B, nh // HG, nc),
        in_specs=[pl.BlockSpec((C, W), lambda b, h, c: (b * nc + c, h)),
                  pl.BlockSpec((C, W), lambda b, h, c: (b * nc + c, npart + h)),
                  pl.BlockSpec((C, W), lambda b, h, c: (b * nc + c, 2 * npart + h)),
                  pl.BlockSpec((C, W), lambda b, h, c: (zr0 + b * nc + c, h)),
                  pl.BlockSpec((C, 2 * nh), lambda b, h, c: (zr0 + b * nc + c, 0)),
                  vec(nh), vec(nh), vec(HEAD_DIM),
                  pl.BlockSpec((1, HG, HEAD_DIM, HEAD_DIM), lambda b, h, c: (b, h, 0, 0))],
        out_specs=[pl.BlockSpec((C, W), lambda b, h, c: (b * nc + c, h)),
                   pl.BlockSpec((1, HG, HEAD_DIM, HEAD_DIM), lambda b, h, c: (b, h, 0, 0))],
        out_shape=[jax.ShapeDtypeStruct((B * L, D), BF16),
                   jax.ShapeDtypeStruct((B, nh, HEAD_DIM, HEAD_DIM), F32)],
        scratch_shapes=[pltpu.VMEM((LANES, LANES), F32)],
        compiler_params=_params("parallel", "parallel", "arbitrary"),
    )(qkvn, qkvn, qkvn, z, ab, A_log.reshape(1, nh), dt_bias.reshape(1, nh),
      norm_w.reshape(1, HEAD_DIM), S0)


def _rope_tables(B, L, Bs, Ls, past_len):
    pos = jnp.concatenate([jnp.tile(jnp.arange(L), B), jnp.tile(past_len + jnp.arange(Ls), Bs)])
    inv = ROPE_THETA ** (-jnp.arange(0, HEAD_DIM, 2, dtype=F32) / HEAD_DIM)
    ang = pos.astype(F32)[:, None] * inv[None, :]
    cos = jnp.cos(ang)
    sin = jnp.sin(ang)
    return jnp.concatenate([cos, cos], axis=1), jnp.concatenate([-sin, sin], axis=1)


def kernel(x_prompt, x_sample, cache_k_diff, cache_v_diff, cache_k_fox, cache_v_fox, cache_logf_fox, state_pool, state_gdn_conv, state_gdn_S, page_table, p_prompt, p_sample, ln1_g, ln1_b, ln2_g, ln2_b, w_mlp_up, w_mlp_down, w_ple_gate, w_ple_proj, w_qkv_diff, lam_q1, lam_k1, lam_q2, lam_k2, subln_diff, w_o_diff, w_in_fox, b_f_fox, w_o_fox, w_pool, pool_scale, w_in_gdn, conv_gdn, A_log_gdn, dt_bias_gdn, norm_gdn, w_o_gdn):
    B, L, D = x_prompt.shape
    Bs, Ls, _ = x_sample.shape
    depth = ln1_g.shape[0]
    Mp, Ms = B * L, Bs * Ls
    M = Mp + Ms
    nh = D // HEAD_DIM
    P = cache_k_diff.shape[1]
    npages = page_table.shape[1]
    past_len = npages * P
    n_phys = cache_k_diff.shape[0]
    alpha = (2 * depth) ** 0.25
    qk_scale = HEAD_DIM ** -0.5
    tm = _pick(M, 1376, 16)

    x = jnp.concatenate([x_prompt.reshape(Mp, D), x_sample.reshape(Ms, D)], axis=0)
    xb = x.astype(BF16)
    outs = {}

    for i in range(depth):
        kind = i % N_MIXERS
        if kind == 0:
            lambda_init = 0.8 - 0.6 * math.exp(-0.3 * i)
            cos, sin = _rope_tables(B, L, Bs, Ls, past_len)
            rope_extras = [(cos, (tm, HEAD_DIM), lambda i_, j, k: (i_, 0)),
                           (sin, (tm, HEAD_DIM), lambda i_, j, k: (i_, 0))]
            (q,), _ = _matmul(xb, w_qkv_diff, col_off=0, n_cols=D, out_dtypes=[BF16],
                              epilogue=functools.partial(_epi_rope, scale=qk_scale),
                              extras=rope_extras)
            (k32, kb), _ = _matmul(xb, w_qkv_diff, col_off=D, n_cols=D, out_dtypes=[F32, BF16],
                                   epilogue=functools.partial(_epi_rope, scale=1.0),
                                   extras=rope_extras)
            v32, vb = _mm_plain(xb, w_qkv_diff, [F32, BF16], col_off=2 * D, n_cols=D)
            lam4 = jnp.stack([lam_q1, lam_k1, lam_q2, lam_k2]).astype(F32)
            extra = (lam4, subln_diff.reshape(1, 2 * HEAD_DIM))
            o_p = _flash_prompt(q, kb, vb, B=B, L=L, mode="diff", extra=extra,
                                lambda_init=lambda_init)
            o_s = _decode_attention(q[Mp:].reshape(Bs, Ls, D), kb[Mp:].reshape(Bs, Ls, D),
                                    vb[Mp:].reshape(Bs, Ls, D),
                                    cache_k_diff.reshape(n_phys, P, D),
                                    cache_v_diff.reshape(n_phys, P, D), page_table,
                                    mode="diff", extra=extra, lambda_init=lambda_init)
            o = jnp.concatenate([o_p, o_s.reshape(Ms, D)], axis=0)
            y = _mm_residual(o, w_o_diff, x, alpha)
            outs["k_diff_p"] = k32[:Mp].reshape(B, L, nh, HEAD_DIM)
            outs["v_diff_p"] = v32[:Mp].reshape(B, L, nh // 2, 2 * HEAD_DIM)
            outs["k_diff_s"] = k32[Mp:].reshape(Bs, Ls, nh, HEAD_DIM)
            outs["v_diff_s"] = v32[Mp:].reshape(Bs, Ls, nh // 2, 2 * HEAD_DIM)
        elif kind == 1:
            (q,), _ = _matmul(xb, w_in_fox, col_off=0, n_cols=D, out_dtypes=[BF16],
                              epilogue=functools.partial(_epi_scale, scale=qk_scale))
            k32, kb = _mm_plain(xb, w_in_fox, [F32, BF16], col_off=D, n_cols=D)
            v32, vb = _mm_plain(xb, w_in_fox, [F32, BF16], col_off=2 * D, n_cols=D)
            (logf,), _ = _matmul(xb, w_in_fox[:, 3 * D:], epilogue=_epi_logsigmoid,
                                 out_dtypes=[F32],
                                 extras=[(b_f_fox.reshape(1, nh), (1, nh), lambda i_, j, k: (0, 0))])
            tb = P
            tbl_p = jnp.arange(Mp // tb, dtype=jnp.int32).reshape(B, L // tb)
            c_p = _cumsum_blocks(logf[:Mp].reshape(Mp // tb, tb, nh), tbl_p,
                                 jnp.zeros((B, 1, nh), F32)).reshape(Mp, nh)
            c_c = _cumsum_blocks(cache_logf_fox, page_table, jnp.zeros((Bs, 1, nh), F32))
            logf_s = logf[Mp:].reshape(Bs, Ls, nh)
            c_n = _cumsum_blocks(jnp.pad(logf_s, ((0, 0), (0, tb - Ls), (0, 0))),
                                 jnp.arange(Bs, dtype=jnp.int32).reshape(Bs, 1),
                                 c_c[:, past_len - 1:past_len, :])
            cqh = c_p.reshape(Mp, nh // 2, 2).transpose(1, 0, 2)
            ckh = c_p.T.reshape(nh // 2, 2, Mp)
            o_p = _flash_prompt(q, kb, vb, B=B, L=L, mode="fox", extra=(cqh, ckh))
            extra = (c_n[:, :Ls, :], c_c.transpose(0, 2, 1), c_n.transpose(0, 2, 1))
            o_s = _decode_attention(q[Mp:].reshape(Bs, Ls, D), kb[Mp:].reshape(Bs, Ls, D),
                                    vb[Mp:].reshape(Bs, Ls, D),
                                    cache_k_fox.reshape(n_phys, P, D),
                                    cache_v_fox.reshape(n_phys, P, D), page_table,
                                    mode="fox", extra=extra)
            o = jnp.concatenate([o_p, o_s.reshape(Ms, D)], axis=0)
            y = _mm_residual(o, w_o_fox, x, alpha)
            outs["k_fox_p"] = k32[:Mp].reshape(B, L, nh, HEAD_DIM)
            outs["v_fox_p"] = v32[:Mp].reshape(B, L, nh, HEAD_DIM)
            outs["logf_fox_p"] = logf[:Mp].reshape(B, L, nh)
            outs["k_fox_s"] = k32[Mp:].reshape(Bs, Ls, nh, HEAD_DIM)
            outs["v_fox_s"] = v32[Mp:].reshape(Bs, Ls, nh, HEAD_DIM)
            outs["logf_fox_s"] = logf_s
        elif kind == 2:
            keep = state_pool.shape[1]
            prev_p = jnp.zeros((B, POOL_HALO, D), F32)
            prev_s = jnp.pad(state_pool, ((0, 0), (POOL_HALO - keep, 0), (0, 0)))
            y_p = _pool_mixer(x, prev_p, w_pool, pool_scale, row0=0, B=B, L=L, start=0, alpha=alpha)
            y_s = _pool_mixer(x, prev_s, w_pool, pool_scale, row0=Mp, B=Bs, L=Ls, start=past_len,
                              alpha=alpha)
            y = jnp.concatenate([y_p, y_s], axis=0)
            xs3 = x[Mp:].reshape(Bs, Ls, D)
            outs["pool_p"] = x[:Mp].reshape(B, L, D)[:, L - keep:]
            outs["pool_s"] = jnp.concatenate([state_pool, xs3], axis=1)[:, -keep:]
        else:
            keep = state_gdn_conv.shape[1]
            qkv = _mm_plain(xb, w_in_gdn, [F32], col_off=0, n_cols=3 * D)[0]
            z = _mm_plain(xb, w_in_gdn, [F32], col_off=3 * D, n_cols=D)[0]
            ab = _mm_plain(xb, w_in_gdn[:, 4 * D:], [F32])[0]
            prev_p = jnp.zeros((B, CONV_HALO, 3 * D), F32)
            prev_s = jnp.pad(state_gdn_conv, ((0, 0), (CONV_HALO - keep, 0), (0, 0)))
            cn_p = _gdn_conv(qkv, prev_p, conv_gdn, row0=0, B=B, L=L, D=D)
            cn_s = _gdn_conv(qkv, prev_s, conv_gdn, row0=Mp, B=Bs, L=Ls, D=D)
            C = GDN_CHUNK
            o_p, S_p = _gdn_core(cn_p, z, ab, jnp.zeros((B, nh, HEAD_DIM, HEAD_DIM), F32),
                                 A_log_gdn, dt_bias_gdn, norm_gdn, B=B, L=L, valid_len=L, zrow0=0)
            padrows = lambda a: jnp.pad(a.reshape(Bs, Ls, -1), ((0, 0), (0, C - Ls), (0, 0))
                                        ).reshape(Bs * C, -1)
            o_s, S_s = _gdn_core(padrows(cn_s), padrows(z[Mp:]), padrows(ab[Mp:]), state_gdn_S,
                                 A_log_gdn, dt_bias_gdn, norm_gdn, B=Bs, L=C, valid_len=Ls, zrow0=0)
            o = jnp.concatenate([o_p, o_s.reshape(Bs, C, D)[:, :Ls].reshape(Ms, D)], axis=0)
            y = _mm_residual(o, w_o_gdn, x, alpha)
            qkv_p = qkv[:Mp].reshape(B, L, 3 * D)
            qkv_s = qkv[Mp:].reshape(Bs, Ls, 3 * D)
            outs["conv_p"] = qkv_p[:, L - keep:]
            outs["conv_s"] = jnp.concatenate([state_gdn_conv, qkv_s], axis=1)[:, -keep:]
            outs["S_p"] = S_p
            outs["S_s"] = S_s

        x, xb = _layer_norm(y, ln1_g[i], ln1_b[i])
        (h,), _ = _matmul(xb, w_mlp_up[i], epilogue=_epi_relu2, out_dtypes=[BF16])
        y = _mm_residual(h, w_mlp_down[i], x, alpha)
        x, xb = _layer_norm(y, ln2_g[i], ln2_b[i])
        pcat = jnp.concatenate([p_prompt[i].reshape(Mp, -1), p_sample[i].reshape(Ms, -1)], axis=0)
        pdim = pcat.shape[1]
        tn = _pick(D, 512, LANES)
        (x, xb), _ = _matmul(
            xb, w_ple_gate[i], epilogue=_epi_ple, out_dtypes=[F32, BF16], tn_target=512,
            extras=[_tile_extra(x, tm, tn),
                    (pcat, (tm, pdim), lambda i_, j, k: (i_, 0)),
                    (w_ple_proj[i], (pdim, tn), lambda i_, j, k: (0, j))])

    return (x[:Mp].reshape(B, L, D), x[Mp:].reshape(Bs, Ls, D),
            outs["k_diff_p"], outs["v_diff_p"], outs["k_diff_s"], outs["v_diff_s"],
            outs["k_fox_p"], outs["v_fox_p"], outs["logf_fox_p"],
            outs["k_fox_s"], outs["v_fox_s"], outs["logf_fox_s"],
            outs["pool_p"], outs["pool_s"], outs["conv_p"], outs["S_p"],
            outs["conv_s"], outs["S_s"])
```

```python
import functools
import math

import jax
import jax.numpy as jnp
from jax import lax
from jax.experimental import pallas as pl
from jax.experimental.pallas import tpu as pltpu

F32 = jnp.float32
BF16 = jnp.bfloat16

HEAD_DIM = 128
POOL_WINDOWS = (2, 4, 8, 16)
POOL_HALO = 16
CONV_HALO = 8
GDN_CHUNK = 64
GDN_HEAD_GROUP = 8
GDN_STACK = 4
GDN_HI_ROUNDS = 2
ROPE_THETA = 10000.0
LN_EPS = 1e-5
RMS_EPS = 1e-5
N_MIXERS = 4
LANES = 128
SUBLANES = 8
VMEM_LIMIT = 56 * 1024 * 1024

_NT = (((1,), (1,)), ((), ()))
_TN = (((0,), (0,)), ((), ()))


def _params(*sem):
    return pltpu.CompilerParams(dimension_semantics=sem, vmem_limit_bytes=VMEM_LIMIT)


def _pick(dim, target, align):
    best = None
    for t in range(align, min(dim, target) + 1, align):
        if dim % t == 0:
            best = t
    return best if best is not None else dim


def _sigmoid(x):
    return 1.0 / (1.0 + jnp.exp(-x))


def _softplus(x):
    return jnp.maximum(x, 0.0) + jnp.log1p(jnp.exp(-jnp.abs(x)))


def _split3(a):
    hi = a.astype(BF16)
    r = a - hi.astype(F32)
    mid = r.astype(BF16)
    lo = (r - mid.astype(F32)).astype(BF16)
    return hi, mid, lo


def _split2(a):
    hi = a.astype(BF16)
    return hi, (a - hi.astype(F32)).astype(BF16)


def _dot(a, b):
    return jnp.dot(a, b, preferred_element_type=F32)


def _dot_hi(a, b):
    a1, a2 = _split2(a)
    b1, b2 = _split2(b)
    return _dot(a1, b1) + (_dot(a1, b2) + _dot(a2, b1))


def _dot_lo(a, b):
    return _dot(a.astype(BF16), b.astype(BF16))


def _tri_dot_hi(tri_bf16, x):
    x1, x2, x3 = _split3(x)
    return _dot(tri_bf16, x1) + _dot(tri_bf16, x2) + _dot(tri_bf16, x3)


def _mm_body(*refs, nk, n_extra, n_out, epilogue):
    x_ref, w_ref = refs[0], refs[1]
    extra = refs[2:2 + n_extra]
    outs = refs[2 + n_extra:2 + n_extra + n_out]
    if nk == 1:
        epilogue(_dot(x_ref[...].astype(BF16), w_ref[...].astype(BF16)), extra, outs)
        return
    acc_ref = refs[-1]
    k = pl.program_id(2)

    @pl.when(k == 0)
    def _():
        acc_ref[...] = jnp.zeros_like(acc_ref)

    acc_ref[...] += _dot(x_ref[...].astype(BF16), w_ref[...].astype(BF16))

    @pl.when(k == nk - 1)
    def _():
        epilogue(acc_ref[...], extra, outs)


def _mm_tiles(M, K, n_cols, tm_target=1376, tn_target=512, tk_target=None):
    if tk_target is None:
        tk_target = K if K <= 4096 else 2048
    return _pick(M, tm_target, 16), _pick(n_cols, tn_target, LANES), _pick(K, tk_target, LANES)


def _matmul(x, w, *, epilogue, out_dtypes, col_off=0, n_cols=None, extras=(), out_lane=None,
            name="mm", **tile_kw):
    M, K = x.shape
    n_cols = w.shape[1] - col_off if n_cols is None else n_cols
    tm, tn, tk = _mm_tiles(M, K, n_cols, **tile_kw)
    assert col_off % tn == 0
    joff = col_off // tn
    nk = K // tk
    grid = (M // tm, n_cols // tn, nk)
    xspec = (pl.BlockSpec((tm, tk), lambda i, j, k: (i, k), pipeline_mode=pl.Buffered(1))
             if nk == 1 else pl.BlockSpec((tm, tk), lambda i, j, k: (i, k)))
    in_specs = [xspec, pl.BlockSpec((tk, tn), lambda i, j, k: (k, j + joff))]
    args = [x, w]
    for arr, bshape, imap in extras:
        in_specs.append(pl.BlockSpec(bshape, imap))
        args.append(arr)
    out_lane = [None] * len(out_dtypes) if out_lane is None else out_lane
    out_shape, out_specs = [], []
    for dt, dh in zip(out_dtypes, out_lane):
        if dh is None:
            out_shape.append(jax.ShapeDtypeStruct((M, n_cols), dt))
            out_specs.append(pl.BlockSpec((tm, tn), lambda i, j, k: (i, j)))
        else:
            out_shape.append(jax.ShapeDtypeStruct((M, n_cols // dh, dh), dt))
            out_specs.append(pl.BlockSpec((tm, tn // dh, dh), lambda i, j, k: (i, j, 0)))
    body = functools.partial(_mm_body, nk=nk, n_extra=len(extras), n_out=len(out_dtypes),
                             epilogue=epilogue)
    return pl.pallas_call(
        body, grid=grid, in_specs=in_specs, out_specs=out_specs, out_shape=out_shape,
        scratch_shapes=[pltpu.VMEM((tm, tn), F32)] if nk > 1 else [],
        compiler_params=_params("parallel", "parallel", "arbitrary"), name=name,
    )(*args)


def _put(o_ref, val):
    o_ref[...] = val.astype(o_ref.dtype).reshape(o_ref.shape)


def _epi_store(acc, extra, outs):
    for o in outs:
        _put(o, acc)


def _epi_rope(acc, extra, outs, *, scale):
    cos = extra[0][...]
    sin = extra[1][...]
    parts = []
    for c in range(acc.shape[1] // HEAD_DIM):
        xc = acc[:, c * HEAD_DIM:(c + 1) * HEAD_DIM]
        parts.append(xc * cos + pltpu.roll(xc, HEAD_DIM // 2, 1) * sin)
    r = jnp.concatenate(parts, axis=1)
    if scale != 1.0:
        r = r * scale
    for o in outs:
        _put(o, r)


def _epi_scale(acc, extra, outs, *, scale):
    for o in outs:
        _put(o, acc * scale)


def _epi_relu2(acc, extra, outs):
    h = jnp.maximum(acc, 0.0)
    _put(outs[0], h * h)


def _epi_residual(acc, extra, outs, *, alpha):
    _put(outs[0], alpha * extra[0][...] + acc)


def _epi_ple(acc, extra, outs):
    x_ref, p_ref, wp_ref = extra
    pp = _dot(p_ref[...].astype(BF16), wp_ref[...].astype(BF16))
    y = x_ref[...] + _sigmoid(acc) * pp
    for o in outs:
        _put(o, y)


def _epi_logsigmoid(acc, extra, outs):
    z = acc + extra[0][...]
    _put(outs[0], -(jnp.maximum(-z, 0.0) + jnp.log1p(jnp.exp(-jnp.abs(z)))))


def _tile_extra(arr, tm, tn):
    return (arr, (tm, tn), lambda i, j, k: (i, j))


def _mm_plain(x, w, out_dtypes, **kw):
    return _matmul(x, w, epilogue=_epi_store, out_dtypes=out_dtypes, **kw)


def _mm_residual(x, w, resid, alpha, name):
    M, N = resid.shape
    tm, tn, _ = _mm_tiles(M, x.shape[1], N)
    return _matmul(x, w, epilogue=functools.partial(_epi_residual, alpha=alpha), out_dtypes=[F32],
                   extras=[_tile_extra(resid, tm, tn)], name=name)[0]


def _ln_body(y_ref, g_ref, b_ref, o32_ref, o16_ref):
    y = y_ref[...]
    mu = jnp.mean(y, axis=-1, keepdims=True)
    d = y - mu
    var = jnp.mean(d * d, axis=-1, keepdims=True)
    out = d * lax.rsqrt(var + LN_EPS) * g_ref[...] + b_ref[...]
    o32_ref[...] = out
    o16_ref[...] = out.astype(BF16)


def _layer_norm(y, g, b):
    M, D = y.shape
    tm = _pick(M, 256, 16)
    row = pl.BlockSpec((tm, D), lambda i: (i, 0))
    vec = pl.BlockSpec((1, D), lambda i: (0, 0))
    return pl.pallas_call(
        _ln_body, grid=(M // tm,), in_specs=[row, vec, vec], out_specs=[row, row],
        out_shape=[jax.ShapeDtypeStruct((M, D), F32), jax.ShapeDtypeStruct((M, D), BF16)],
        compiler_params=_params("parallel"), name="layer_norm",
    )(y, g.reshape(1, D), b.reshape(1, D))


def _lambda_full(lam_ref, lambda_init):
    lam = lam_ref[...]
    e1 = jnp.exp(jnp.sum(lam[0:1] * lam[1:2], axis=-1, keepdims=True))
    e2 = jnp.exp(jnp.sum(lam[2:3] * lam[3:4], axis=-1, keepdims=True))
    return e1 - e2 + lambda_init


def _diff_finish(o0, o1, lam, subw, lambda_init):
    o = o0 - lam * o1
    o = o * lax.rsqrt(jnp.mean(o * o, axis=-1, keepdims=True) + RMS_EPS) * subw
    return o * (1.0 - lambda_init)


def _flash_body(*refs, mode, nk, tq, tk, lambda_init):
    if mode == "diff":
        q_ref, k_ref, v_ref, lam_ref, subw_ref, o_ref, m_scr, l_scr, acc_scr = refs
    else:
        q_ref, k_ref, v_ref, cq_ref, ck_ref, o_ref, m_scr, l_scr, acc_scr = refs
    qi = pl.program_id(2)
    ki = pl.program_id(3)

    @pl.when(ki == 0)
    def _():
        m_scr[...] = jnp.full(m_scr.shape, -jnp.inf, F32)
        l_scr[...] = jnp.zeros(l_scr.shape, F32)
        acc_scr[...] = jnp.zeros(acc_scr.shape, F32)

    def update(on_diagonal):
        q = q_ref[...]
        k = k_ref[...]
        v = v_ref[...]
        if on_diagonal:
            mask = (lax.broadcasted_iota(jnp.int32, (tq, tk), 1)
                    <= lax.broadcasted_iota(jnp.int32, (tq, tk), 0))
        for c in range(2):
            sl = slice(c * HEAD_DIM, (c + 1) * HEAD_DIM)
            s = lax.dot_general(q[:, sl], k[:, sl], _NT, preferred_element_type=F32)
            if mode == "fox":
                s = s + (cq_ref[0][:, c:c + 1] - ck_ref[0][c:c + 1, :])
            if on_diagonal:
                s = jnp.where(mask, s, -jnp.inf)
            m_old = m_scr[c]
            m_new = jnp.maximum(m_old, jnp.max(s, axis=-1, keepdims=True))
            alpha = jnp.exp(m_old - m_new)
            p = jnp.exp(s - m_new)
            l_scr[c] = alpha * l_scr[c] + jnp.sum(p, axis=-1, keepdims=True)
            vv = v if mode == "diff" else v[:, sl]
            acc_scr[c] = alpha * acc_scr[c] + _dot(p.astype(BF16), vv)
            m_scr[c] = m_new

    @pl.when(ki < qi)
    def _():
        update(False)

    @pl.when(ki == qi)
    def _():
        update(True)

    @pl.when(ki == nk - 1)
    def _():
        o0 = acc_scr[0] / l_scr[0]
        o1 = acc_scr[1] / l_scr[1]
        if mode == "diff":
            lam = _lambda_full(lam_ref, lambda_init)
            o = _diff_finish(o0, o1, lam, subw_ref[...], lambda_init)
        else:
            o = jnp.concatenate([o0, o1], axis=1)
        o_ref[...] = o.astype(o_ref.dtype)


def _flash_prompt(q, k, v, *, B, L, mode, extra, lambda_init=0.0):
    D = q.shape[1]
    W = 2 * HEAD_DIM
    tq = tk = _pick(L, 512, LANES)
    nq = L // tq
    grid = (B, D // W, nq, nq)
    qspec = pl.BlockSpec((tq, W), lambda b, h, qi, ki: (b * nq + qi, h))
    kspec = pl.BlockSpec((tk, W), lambda b, h, qi, ki: (b * nq + jnp.minimum(ki, qi), h))
    if mode == "diff":
        lam4, subw = extra
        especs = [pl.BlockSpec(lam4.shape, lambda b, h, qi, ki: (0, 0)),
                  pl.BlockSpec(subw.shape, lambda b, h, qi, ki: (0, 0))]
        dv = W
    else:
        cqh, ckh = extra
        especs = [pl.BlockSpec((1, tq, 2), lambda b, h, qi, ki: (h, b * nq + qi, 0)),
                  pl.BlockSpec((1, 2, tk), lambda b, h, qi, ki: (h, 0, b * nq + jnp.minimum(ki, qi)))]
        dv = HEAD_DIM
    body = functools.partial(_flash_body, mode=mode, nk=nq, tq=tq, tk=tk, lambda_init=lambda_init)
    return pl.pallas_call(
        body, grid=grid, in_specs=[qspec, kspec, kspec] + especs,
        out_specs=pl.BlockSpec((tq, W), lambda b, h, qi, ki: (b * nq + qi, h)),
        out_shape=jax.ShapeDtypeStruct((B * L, D), BF16),
        scratch_shapes=[pltpu.VMEM((2, tq, 1), F32), pltpu.VMEM((2, tq, 1), F32),
                        pltpu.VMEM((2, tq, dv), F32)],
        compiler_params=_params("parallel", "parallel", "parallel", "arbitrary"),
        name="flash_" + mode,
    )(q, k, v, *extra)


def _decode_body(*refs, mode, npages, nh, ls, lambda_init):
    pt_ref = refs[0]
    if mode == "diff":
        (q_ref, kc_ref, vc_ref, kn_ref, vn_ref, lam_ref, subw_ref,
         o_ref, m_scr, l_scr, acc_scr) = refs[1:]
    else:
        (q_ref, kc_ref, vc_ref, kn_ref, vn_ref, cq_ref, ckc_ref, ckn_ref,
         o_ref, m_scr, l_scr, acc_scr) = refs[1:]
    del pt_ref
    p = pl.program_id(1)
    P = kc_ref.shape[1]
    dv = acc_scr.shape[-1]

    @pl.when(p == 0)
    def _():
        m_scr[...] = jnp.full(m_scr.shape, -jnp.inf, F32)
        l_scr[...] = jnp.zeros(l_scr.shape, F32)
        acc_scr[...] = jnp.zeros(acc_scr.shape, F32)

    def step(k_of, v_of, ck, mask):
        q = q_ref[0]
        probs = []
        for h in range(nh):
            s = lax.dot_general(q[:, h * HEAD_DIM:(h + 1) * HEAD_DIM], k_of(h), _NT,
                                preferred_element_type=F32)
            if mode == "fox":
                s = s + (cq_ref[0][:, h:h + 1] - ck[h:h + 1, :])
            if mask is not None:
                s = jnp.where(mask, s, -jnp.inf)
            m_old = m_scr[h]
            m_new = jnp.maximum(m_old, jnp.max(s, axis=-1, keepdims=True))
            alpha = jnp.exp(m_old - m_new)
            pr = jnp.exp(s - m_new)
            l_scr[h] = alpha * l_scr[h] + jnp.sum(pr, axis=-1, keepdims=True)
            m_scr[h] = m_new
            probs.append((alpha, pr.astype(BF16)))
        if mode == "diff":
            for hh in range(nh // 2):
                (a0, p0), (a1, p1) = probs[2 * hh], probs[2 * hh + 1]
                pv = _dot(jnp.concatenate([p0, p1], axis=0), v_of(hh))
                al = jnp.concatenate([a0, a1], axis=0)
                acc_scr[hh] = al * acc_scr[hh] + pv
        else:
            for h in range(nh):
                a0, p0 = probs[h]
                acc_scr[h] = a0 * acc_scr[h] + _dot(p0, v_of(h))

    @pl.when(p < npages)
    def _():
        kc = pltpu.einshape("khd->hkd", kc_ref[0].astype(BF16))
        vc = pltpu.einshape("khd->hkd", vc_ref[0].astype(BF16))
        ck = ckc_ref[0] if mode == "fox" else None
        step(lambda h: kc[h], lambda h: vc[h], ck, None)

    @pl.when(p == npages)
    def _():
        row = lax.broadcasted_iota(jnp.int32, (ls, P), 0)
        col = lax.broadcasted_iota(jnp.int32, (ls, P), 1)
        ck = ckn_ref[0] if mode == "fox" else None
        kn = kn_ref[0]
        vn = vn_ref[0]
        step(lambda h: kn[:, h * HEAD_DIM:(h + 1) * HEAD_DIM],
             lambda h: vn[:, h * dv:(h + 1) * dv], ck, col <= row)
        outs = []
        if mode == "diff":
            lam = _lambda_full(lam_ref, lambda_init)
            for hh in range(nh // 2):
                acc = acc_scr[hh]
                o0 = acc[:ls] / l_scr[2 * hh]
                o1 = acc[ls:] / l_scr[2 * hh + 1]
                outs.append(_diff_finish(o0, o1, lam, subw_ref[...], lambda_init))
        else:
            for h in range(nh):
                outs.append(acc_scr[h] / l_scr[h])
        o_ref[0] = jnp.concatenate(outs, axis=1).astype(o_ref.dtype)


def _decode_attention(q_s, k_new, v_new, cache_k, cache_v, page_table, *, mode, extra,
                      lambda_init=0.0):
    Bs, Ls, D = q_s.shape
    P = cache_k.shape[1]
    npages = page_table.shape[1]
    nh = D // HEAD_DIM
    pad = ((0, 0), (0, P - Ls), (0, 0))
    kn = jnp.pad(k_new, pad)
    vn = jnp.pad(v_new, pad)
    page = lambda b, p, pt: (pt[b, jnp.minimum(p, npages - 1)], 0, 0, 0)
    per_b = lambda b, p, pt: (b, 0, 0)
    in_specs = [pl.BlockSpec((1, Ls, D), per_b),
                pl.BlockSpec((1,) + cache_k.shape[1:], page),
                pl.BlockSpec((1,) + cache_v.shape[1:], page),
                pl.BlockSpec((1, P, D), per_b), pl.BlockSpec((1, P, D), per_b)]
    if mode == "diff":
        lam4, subw = extra
        in_specs += [pl.BlockSpec(lam4.shape, lambda b, p, pt: (0, 0)),
                     pl.BlockSpec(subw.shape, lambda b, p, pt: (0, 0))]
        acc_shape = (nh // 2, 2 * Ls, 2 * HEAD_DIM)
    else:
        cq, ckc, ckn = extra
        in_specs += [pl.BlockSpec((1, Ls, nh), per_b),
                     pl.BlockSpec((1, nh, P), lambda b, p, pt: (b, 0, jnp.minimum(p, npages - 1))),
                     pl.BlockSpec((1, nh, P), per_b)]
        acc_shape = (nh, Ls, HEAD_DIM)
    body = functools.partial(_decode_body, mode=mode, npages=npages, nh=nh, ls=Ls,
                             lambda_init=lambda_init)
    grid_spec = pltpu.PrefetchScalarGridSpec(
        num_scalar_prefetch=1, grid=(Bs, npages + 1), in_specs=in_specs,
        out_specs=pl.BlockSpec((1, Ls, D), per_b),
        scratch_shapes=[pltpu.VMEM((nh, Ls, 1), F32), pltpu.VMEM((nh, Ls, 1), F32),
                        pltpu.VMEM(acc_shape, F32)])
    return pl.pallas_call(
        body, grid_spec=grid_spec, out_shape=jax.ShapeDtypeStruct((Bs, Ls, D), BF16),
        compiler_params=_params("parallel", "arbitrary"), name="decode_" + mode,
    )(page_table, q_s, cache_k, cache_v, kn, vn, *extra)


def _cumsum_body(tbl_ref, x_ref, init_ref, o_ref, carry, *, tb):
    del tbl_ref
    j = pl.program_id(1)

    @pl.when(j == 0)
    def _():
        carry[...] = init_ref[0]

    r = lax.broadcasted_iota(jnp.int32, (tb, tb), 0)
    c = lax.broadcasted_iota(jnp.int32, (tb, tb), 1)
    tri = (r >= c).astype(BF16)
    out = _tri_dot_hi(tri, x_ref[0]) + carry[...]
    o_ref[0] = out
    carry[...] = out[tb - 1:tb, :]


def _cumsum_blocks(src, table, init):
    _, tb, H = src.shape
    S, nb = table.shape
    grid_spec = pltpu.PrefetchScalarGridSpec(
        num_scalar_prefetch=1, grid=(S, nb),
        in_specs=[pl.BlockSpec((1, tb, H), lambda s, j, t: (t[s, j], 0, 0)),
                  pl.BlockSpec((1, 1, H), lambda s, j, t: (s, 0, 0))],
        out_specs=pl.BlockSpec((1, tb, H), lambda s, j, t: (s, j, 0)),
        scratch_shapes=[pltpu.VMEM((1, H), F32)])
    return pl.pallas_call(
        functools.partial(_cumsum_body, tb=tb), grid_spec=grid_spec,
        out_shape=jax.ShapeDtypeStruct((S, nb * tb, H), F32),
        compiler_params=_params("parallel", "arbitrary"), name="logf_cumsum",
    )(table, src, init)


def _pool_body(x_ref, prev_ref, w_ref, scale_ref, o_ref, carry, *, tl, start, alpha):
    g = pl.program_id(1)
    t = pl.program_id(2)

    @pl.when(t == 0)
    def _():
        carry[...] = prev_ref[0]

    x = x_ref[...]
    xe = jnp.concatenate([carry[...], x], axis=0)
    s2 = xe + pltpu.roll(xe, 1, 0)
    s4 = s2 + pltpu.roll(s2, 2, 0)
    s8 = s4 + pltpu.roll(s4, 4, 0)
    s16 = s8 + pltpu.roll(s8, 8, 0)
    win = jnp.where(g == 0, s2, jnp.where(g == 1, s4, jnp.where(g == 2, s8, s16)))[POOL_HALO:]
    width = jnp.left_shift(2, g)
    pos = start + t * tl + lax.broadcasted_iota(jnp.int32, (tl, 1), 0)
    cnt = jnp.minimum(pos + 1, width).astype(F32)
    mixed = win / cnt - x
    y = _dot(mixed.astype(BF16), w_ref[0].astype(BF16)) * scale_ref[...]
    o_ref[...] = alpha * x + y
    carry[...] = xe[tl:tl + POOL_HALO]


def _pool_mixer(x, prev, w_pool, pool_scale, *, row0, B, L, start, alpha):
    D = x.shape[1]
    ng = len(POOL_WINDOWS)
    G = D // ng
    tl = _pick(L, 512, SUBLANES)
    nt = L // tl
    r0 = row0 // tl
    assert row0 % tl == 0
    body = functools.partial(_pool_body, tl=tl, start=start, alpha=alpha)
    return pl.pallas_call(
        body, grid=(B, ng, nt),
        in_specs=[pl.BlockSpec((tl, G), lambda b, g, t: (r0 + b * nt + t, g)),
                  pl.BlockSpec((1, POOL_HALO, G), lambda b, g, t: (b, 0, g)),
                  pl.BlockSpec((1, G, G), lambda b, g, t: (g, 0, 0)),
                  pl.BlockSpec((1, G), lambda b, g, t: (0, g))],
        out_specs=pl.BlockSpec((tl, G), lambda b, g, t: (b * nt + t, g)),
        out_shape=jax.ShapeDtypeStruct((B * L, D), F32),
        scratch_shapes=[pltpu.VMEM((POOL_HALO, G), F32)],
        compiler_params=_params("parallel", "parallel", "arbitrary"), name="pool_mixer",
    )(x, prev, w_pool, pool_scale.reshape(1, D))


def _conv_body(x_ref, prev_ref, w_ref, o_ref, carry, *, tl, ncb_part, qscale):
    cb = pl.program_id(1)
    t = pl.program_id(2)

    @pl.when(t == 0)
    def _():
        carry[...] = prev_ref[0]

    x = x_ref[...]
    w = w_ref[...]
    xe = jnp.concatenate([carry[...], x], axis=0)
    x1 = pltpu.roll(xe, 1, 0)[CONV_HALO:]
    x2 = pltpu.roll(xe, 2, 0)[CONV_HALO:]
    x3 = pltpu.roll(xe, 3, 0)[CONV_HALO:]
    conv = x3 * w[0:1] + x2 * w[1:2] + x1 * w[2:3] + x * w[3:4]
    act = conv * _sigmoid(conv)
    part = cb // ncb_part
    scale = jnp.where(part == 0, qscale, 1.0)
    pieces = []
    for c in range(act.shape[1] // HEAD_DIM):
        a = act[:, c * HEAD_DIM:(c + 1) * HEAD_DIM]
        n = a * lax.rsqrt(jnp.sum(a * a, axis=-1, keepdims=True) + 1e-6) * scale
        pieces.append(jnp.where(part < 2, n, a))
    o_ref[...] = jnp.concatenate(pieces, axis=1)
    carry[...] = xe[tl:tl + CONV_HALO]


def _gdn_conv(qkv, prev, conv_w, *, row0, B, L, D):
    assert conv_w.shape[0] == 4
    C3 = qkv.shape[1]
    tc = _pick(D, 512, LANES)
    tl = _pick(L, 512, SUBLANES)
    nt = L // tl
    r0 = row0 // tl
    assert row0 % tl == 0
    body = functools.partial(_conv_body, tl=tl, ncb_part=D // tc, qscale=HEAD_DIM ** -0.5)
    return pl.pallas_call(
        body, grid=(B, C3 // tc, nt),
        in_specs=[pl.BlockSpec((tl, tc), lambda b, c, t: (r0 + b * nt + t, c)),
                  pl.BlockSpec((1, CONV_HALO, tc), lambda b, c, t: (b, 0, c)),
                  pl.BlockSpec((4, tc), lambda b, c, t: (0, c))],
        out_specs=pl.BlockSpec((tl, tc), lambda b, c, t: (b * nt + t, c)),
        out_shape=jax.ShapeDtypeStruct((B * L, C3), F32),
        scratch_shapes=[pltpu.VMEM((CONV_HALO, tc), F32)],
        compiler_params=_params("parallel", "parallel", "arbitrary"), name="gdn_conv",
    )(qkv, prev, conv_w)


def _gdn_body(q_ref, k_ref, v_ref, z_ref, ab_ref, alog_ref, dtb_ref, nw_ref, s0_ref,
              o_ref, s_ref, gct_scr, *, C, HG, G4, nh, valid_len):
    hg = pl.program_id(1)
    c = pl.program_id(2)

    @pl.when(c == 0)
    def _():
        s_ref[...] = s0_ref[...]

    ab = ab_ref[...]
    row = c * C + lax.broadcasted_iota(jnp.int32, (C, 1), 0)
    valid = row < valid_len
    beta = jnp.where(valid, _sigmoid(ab[:, :nh]), 0.0)
    g = jnp.where(valid, -jnp.exp(alog_ref[...]) * _softplus(ab[:, nh:] + dtb_ref[...]), 0.0)
    ii = lax.broadcasted_iota(jnp.int32, (C, C), 0)
    jj = lax.broadcasted_iota(jnp.int32, (C, C), 1)
    gc = _tri_dot_hi((ii >= jj).astype(BF16), g)
    g_last = gc[C - 1:C, :]
    e_gc = jnp.exp(gc)
    e_kd = jnp.exp(g_last - gc)
    e_end = jnp.exp(g_last)
    gpad = jnp.concatenate([gc, jnp.zeros((C, LANES - nh), F32)], axis=1)
    gpad = jnp.concatenate([gpad, jnp.zeros((LANES - C, LANES), F32)], axis=0)
    gct_scr[...] = gpad.T
    lane_h = lax.broadcasted_iota(jnp.int32, (1, nh), 1)
    nw = nw_ref[...]
    rounds = max(int(math.ceil(math.log2(C))) - 1, 0)

    W4 = G4 * C
    ri = lax.broadcasted_iota(jnp.int32, (W4, W4), 0)
    ci = lax.broadcasted_iota(jnp.int32, (W4, W4), 1)
    same = (ri // C) == (ci // C)
    incl = same & (ri >= ci)
    strict = same & (ri > ci)
    eye = (ri == ci).astype(F32)
    rows = lambda a, t: a[t * C:(t + 1) * C]

    for grp in range(HG // G4):
        cols, grow, eend = [], [], []
        for t in range(G4):
            head = hg * HG + grp * G4 + t
            sel = lane_h == head
            col = lambda a: jnp.sum(jnp.where(sel, a, 0.0), axis=1, keepdims=True)
            cols.append((col(gc), col(beta), col(e_gc), col(e_kd)))
            eend.append(col(e_end))
            grow.append(gct_scr[pl.ds(head, 1), :][:, :C])
        stack = lambda idx: jnp.concatenate([cl[idx] for cl in cols], axis=0)
        gcol, bcol, egc, ekd = stack(0), stack(1), stack(2), stack(3)
        grow = jnp.concatenate(grow, axis=1)
        sls = [slice((grp * G4 + t) * HEAD_DIM, (grp * G4 + t + 1) * HEAD_DIM) for t in range(G4)]
        q = jnp.concatenate([q_ref[:, sl] for sl in sls], axis=0)
        k = jnp.concatenate([k_ref[:, sl] for sl in sls], axis=0)
        v = jnp.concatenate([v_ref[:, sl] for sl in sls], axis=0)
        kb = k.astype(BF16)
        decay = jnp.exp(jnp.where(incl, gcol - grow, -jnp.inf))
        kk = lax.dot_general(kb, kb, _NT, preferred_element_type=F32)
        aqk = lax.dot_general(q.astype(BF16), kb, _NT, preferred_element_type=F32) * decay
        A = jnp.where(strict, bcol * kk * decay, 0.0)
        X = eye - A
        Pw = A
        for rd in range(rounds):
            mm = _dot_hi if rd < GDN_HI_ROUNDS else _dot_lo
            Pw = mm(Pw, Pw)
            X = X + mm(X, Pw)
        rhs = jnp.concatenate([bcol * v, (bcol * egc) * k], axis=1)
        sol = _dot_hi(X, rhs)
        u = sol[:, :HEAD_DIM]
        wkqg = jnp.concatenate([sol[:, HEAD_DIM:], q * egc], axis=1).astype(BF16)
        kd = (k * ekd).astype(BF16)
        S = [s_ref[0, grp * G4 + t] for t in range(G4)]
        w_parts, o_parts = [], []
        for t in range(G4):
            Sb = S[t].astype(BF16)
            both = _dot(jnp.concatenate([rows(wkqg[:, :HEAD_DIM], t), rows(wkqg[:, HEAD_DIM:], t)],
                                        axis=0), Sb)
            w_parts.append(rows(u, t) - both[:C])
            o_parts.append(both[C:])
        w = jnp.concatenate(w_parts, axis=0)
        wb = w.astype(BF16)
        o = jnp.concatenate(o_parts, axis=0) + _dot(aqk.astype(BF16), wb)
        for t in range(G4):
            s_ref[0, grp * G4 + t] = eend[t] * S[t] + lax.dot_general(
                rows(kd, t), rows(wb, t), _TN, preferred_element_type=F32)
            ot = rows(o, t)
            zz = z_ref[:, sls[t]]
            on = ot * lax.rsqrt(jnp.mean(ot * ot, axis=-1, keepdims=True) + RMS_EPS) * nw
            o_ref[:, sls[t]] = (on * (zz * _sigmoid(zz))).astype(o_ref.dtype)


def _gdn_core(qkvn, z, ab, S0, A_log, dt_bias, norm_w, *, B, L, valid_len, zrow0):
    D = qkvn.shape[1] // 3
    nh = D // HEAD_DIM
    C = GDN_CHUNK
    HG = min(GDN_HEAD_GROUP, nh)
    G4 = min(GDN_STACK, HG)
    assert HG % G4 == 0 and nh % HG == 0
    W = HG * HEAD_DIM
    nc = L // C
    npart = D // W
    zr0 = zrow0 // C
    assert L % C == 0 and zrow0 % C == 0
    body = functools.partial(_gdn_body, C=C, HG=HG, G4=G4, nh=nh, valid_len=valid_len)
    vec = lambda n: pl.BlockSpec((1, n), lambda b, h, c: (0, 0))
    return pl.pallas_call(
        body, grid=(B, nh // HG, nc),
        in_specs=[pl.BlockSpec((C, W), lambda b, h, c: (b * nc + c, h)),
                  pl.BlockSpec((C, W), lambda b, h, c: (b * nc + c, npart + h)),
                  pl.BlockSpec((C, W), lambda b, h, c: (b * nc + c, 2 * npart + h)),
                  pl.BlockSpec((C, W), lambda b, h, c: (zr0 + b * nc + c, h)),
                  pl.BlockSpec((C, 2 * nh), lambda b, h, c: (zr0 + b * nc + c, 0)),
                  vec(nh), vec(nh), vec(HEAD_DIM),
                  pl.BlockSpec((1, HG, HEAD_DIM, HEAD_DIM), lambda b, h, c: (b, h, 0, 0))],
        out_specs=[pl.BlockSpec((C, W), lambda b, h, c: (b * nc + c, h)),
                   pl.BlockSpec((1, HG, HEAD_DIM, HEAD_DIM), lambda b, h, c: (b, h, 0, 0))],
        out_shape=[jax.ShapeDtypeStruct((B * L, D), BF16),
                   jax.ShapeDtypeStruct((B, nh, HEAD_DIM, HEAD_DIM), F32)],
        scratch_shapes=[pltpu.VMEM((LANES, LANES), F32)],
        compiler_params=_params("parallel", "parallel", "arbitrary"), name="gdn_core",
    )(qkvn, qkvn, qkvn, z, ab, A_log.reshape(1, nh), dt_bias.reshape(1, nh),
      norm_w.reshape(1, HEAD_DIM), S0)


def _rope_tables(B, L, Bs, Ls, past_len):
    pos = jnp.concatenate([jnp.tile(jnp.arange(L), B), jnp.tile(past_len + jnp.arange(Ls), Bs)])
    inv = ROPE_THETA ** (-jnp.arange(0, HEAD_DIM, 2, dtype=F32) / HEAD_DIM)
    ang = pos.astype(F32)[:, None] * inv[None, :]
    cos = jnp.cos(ang)
    sin = jnp.sin(ang)
    return jnp.concatenate([cos, cos], axis=1), jnp.concatenate([-sin, sin], axis=1)


def _kv_proj(xb, w, col_off, D, dh, epilogue, extras, name):
    tn = SUBLANES * dh
    tm_target = 1376 if tn <= 1024 else 688
    return _matmul(xb, w, col_off=col_off, n_cols=D, epilogue=epilogue, extras=extras,
                   out_dtypes=[F32, BF16], out_lane=[dh, None], name=name,
                   tm_target=tm_target, tn_target=tn, tk_target=1024)


def kernel(x_prompt, x_sample, cache_k_diff, cache_v_diff, cache_k_fox, cache_v_fox, cache_logf_fox, state_pool, state_gdn_conv, state_gdn_S, page_table, p_prompt, p_sample, ln1_g, ln1_b, ln2_g, ln2_b, w_mlp_up, w_mlp_down, w_ple_gate, w_ple_proj, w_qkv_diff, lam_q1, lam_k1, lam_q2, lam_k2, subln_diff, w_o_diff, w_in_fox, b_f_fox, w_o_fox, w_pool, pool_scale, w_in_gdn, conv_gdn, A_log_gdn, dt_bias_gdn, norm_gdn, w_o_gdn):
    B, L, D = x_prompt.shape
    Bs, Ls, _ = x_sample.shape
    depth = ln1_g.shape[0]
    Mp, Ms = B * L, Bs * Ls
    M = Mp + Ms
    nh = D // HEAD_DIM
    P = cache_k_diff.shape[1]
    npages = page_table.shape[1]
    past_len = npages * P
    alpha = (2 * depth) ** 0.25
    qk_scale = HEAD_DIM ** -0.5

    x = jnp.concatenate([x_prompt.reshape(Mp, D), x_sample.reshape(Ms, D)], axis=0)
    xb = x.astype(BF16)
    outs = {}

    def rope_extras(tm):
        cos, sin = _rope_tables(B, L, Bs, Ls, past_len)
        return [(cos, (tm, HEAD_DIM), lambda i_, j, k: (i_, 0)),
                (sin, (tm, HEAD_DIM), lambda i_, j, k: (i_, 0))]

    for i in range(depth):
        kind = i % N_MIXERS
        if kind == 0:
            lambda_init = 0.8 - 0.6 * math.exp(-0.3 * i)
            dv = 2 * HEAD_DIM
            (q,) = _matmul(xb, w_qkv_diff, col_off=0, n_cols=D, out_dtypes=[BF16],
                           epilogue=functools.partial(_epi_rope, scale=qk_scale),
                           extras=rope_extras(_mm_tiles(M, D, D)[0]), name="diff_q")
            tmk = _mm_tiles(M, D, D, tn_target=SUBLANES * HEAD_DIM, tk_target=1024)[0]
            k4, kb = _kv_proj(xb, w_qkv_diff, D, D, HEAD_DIM,
                              functools.partial(_epi_rope, scale=1.0), rope_extras(tmk), "diff_k")
            v4, vb = _kv_proj(xb, w_qkv_diff, 2 * D, D, dv, _epi_store, (), "diff_v")
            lam4 = jnp.stack([lam_q1, lam_k1, lam_q2, lam_k2]).astype(F32)
            extra = (lam4, subln_diff.reshape(1, dv))
            o_p = _flash_prompt(q, kb, vb, B=B, L=L, mode="diff", extra=extra,
                                lambda_init=lambda_init)
            o_s = _decode_attention(q[Mp:].reshape(Bs, Ls, D), kb[Mp:].reshape(Bs, Ls, D),
                                    vb[Mp:].reshape(Bs, Ls, D), cache_k_diff, cache_v_diff,
                                    page_table, mode="diff", extra=extra, lambda_init=lambda_init)
            o = jnp.concatenate([o_p, o_s.reshape(Ms, D)], axis=0)
            y = _mm_residual(o, w_o_diff, x, alpha, "diff_out")
            outs["k_diff_p"] = k4[:Mp].reshape(B, L, nh, HEAD_DIM)
            outs["v_diff_p"] = v4[:Mp].reshape(B, L, D // dv, dv)
            outs["k_diff_s"] = k4[Mp:].reshape(Bs, Ls, nh, HEAD_DIM)
            outs["v_diff_s"] = v4[Mp:].reshape(Bs, Ls, D // dv, dv)
        elif kind == 1:
            (q,) = _matmul(xb, w_in_fox, col_off=0, n_cols=D, out_dtypes=[BF16],
                           epilogue=functools.partial(_epi_scale, scale=qk_scale), name="fox_q")
            k4, kb = _kv_proj(xb, w_in_fox, D, D, HEAD_DIM, _epi_store, (), "fox_k")
            v4, vb = _kv_proj(xb, w_in_fox, 2 * D, D, HEAD_DIM, _epi_store, (), "fox_v")
            (logf,) = _matmul(xb, w_in_fox[:, 3 * D:], epilogue=_epi_logsigmoid, out_dtypes=[F32],
                              extras=[(b_f_fox.reshape(1, nh), (1, nh), lambda i_, j, k: (0, 0))],
                              name="fox_logf")
            tb = P
            tbl_p = jnp.arange(Mp // tb, dtype=jnp.int32).reshape(B, L // tb)
            c_p = _cumsum_blocks(logf[:Mp].reshape(Mp // tb, tb, nh), tbl_p,
                                 jnp.zeros((B, 1, nh), F32)).reshape(Mp, nh)
            c_c = _cumsum_blocks(cache_logf_fox, page_table, jnp.zeros((Bs, 1, nh), F32))
            logf_s = logf[Mp:].reshape(Bs, Ls, nh)
            c_n = _cumsum_blocks(jnp.pad(logf_s, ((0, 0), (0, tb - Ls), (0, 0))),
                                 jnp.arange(Bs, dtype=jnp.int32).reshape(Bs, 1),
                                 c_c[:, past_len - 1:past_len, :])
            cqh = c_p.reshape(Mp, nh // 2, 2).transpose(1, 0, 2)
            ckh = c_p.T.reshape(nh // 2, 2, Mp)
            o_p = _flash_prompt(q, kb, vb, B=B, L=L, mode="fox", extra=(cqh, ckh))
            extra = (c_n[:, :Ls, :], c_c.transpose(0, 2, 1), c_n.transpose(0, 2, 1))
            o_s = _decode_attention(q[Mp:].reshape(Bs, Ls, D), kb[Mp:].reshape(Bs, Ls, D),
                                    vb[Mp:].reshape(Bs, Ls, D), cache_k_fox, cache_v_fox,
                                    page_table, mode="fox", extra=extra)
            o = jnp.concatenate([o_p, o_s.reshape(Ms, D)], axis=0)
            y = _mm_residual(o, w_o_fox, x, alpha, "fox_out")
            outs["k_fox_p"] = k4[:Mp].reshape(B, L, nh, HEAD_DIM)
            outs["v_fox_p"] = v4[:Mp].reshape(B, L, nh, HEAD_DIM)
            outs["logf_fox_p"] = logf[:Mp].reshape(B, L, nh)
            outs["k_fox_s"] = k4[Mp:].reshape(Bs, Ls, nh, HEAD_DIM)
            outs["v_fox_s"] = v4[Mp:].reshape(Bs, Ls, nh, HEAD_DIM)
            outs["logf_fox_s"] = logf_s
        elif kind == 2:
            keep = state_pool.shape[1]
            prev_p = jnp.zeros((B, POOL_HALO, D), F32)
            prev_s = jnp.pad(state_pool, ((0, 0), (POOL_HALO - keep, 0), (0, 0)))
            y_p = _pool_mixer(x, prev_p, w_pool, pool_scale, row0=0, B=B, L=L, start=0, alpha=alpha)
            y_s = _pool_mixer(x, prev_s, w_pool, pool_scale, row0=Mp, B=Bs, L=Ls, start=past_len,
                              alpha=alpha)
            y = jnp.concatenate([y_p, y_s], axis=0)
            xs3 = x[Mp:].reshape(Bs, Ls, D)
            outs["pool_p"] = x[:Mp].reshape(B, L, D)[:, L - keep:]
            outs["pool_s"] = jnp.concatenate([state_pool, xs3], axis=1)[:, -keep:]
        else:
            keep = state_gdn_conv.shape[1]
            (qkv,) = _mm_plain(xb, w_in_gdn, [F32], col_off=0, n_cols=3 * D, name="gdn_qkv")
            (z,) = _mm_plain(xb, w_in_gdn, [F32], col_off=3 * D, n_cols=D, name="gdn_z")
            (ab,) = _mm_plain(xb, w_in_gdn[:, 4 * D:], [F32], name="gdn_ab")
            prev_p = jnp.zeros((B, CONV_HALO, 3 * D), F32)
            prev_s = jnp.pad(state_gdn_conv, ((0, 0), (CONV_HALO - keep, 0), (0, 0)))
            cn_p = _gdn_conv(qkv, prev_p, conv_gdn, row0=0, B=B, L=L, D=D)
            cn_s = _gdn_conv(qkv, prev_s, conv_gdn, row0=Mp, B=Bs, L=Ls, D=D)
            C = GDN_CHUNK
            o_p, S_p = _gdn_core(cn_p, z, ab, jnp.zeros((B, nh, HEAD_DIM, HEAD_DIM), F32),
                                 A_log_gdn, dt_bias_gdn, norm_gdn, B=B, L=L, valid_len=L, zrow0=0)
            padrows = lambda a: jnp.pad(a.reshape(Bs, Ls, -1), ((0, 0), (0, C - Ls), (0, 0))
                                        ).reshape(Bs * C, -1)
            o_s, S_s = _gdn_core(padrows(cn_s), padrows(z[Mp:]), padrows(ab[Mp:]), state_gdn_S,
                                 A_log_gdn, dt_bias_gdn, norm_gdn, B=Bs, L=C, valid_len=Ls, zrow0=0)
            o = jnp.concatenate([o_p, o_s.reshape(Bs, C, D)[:, :Ls].reshape(Ms, D)], axis=0)
            y = _mm_residual(o, w_o_gdn, x, alpha, "gdn_out")
            qkv_p = qkv[:Mp].reshape(B, L, 3 * D)
            qkv_s = qkv[Mp:].reshape(Bs, Ls, 3 * D)
            outs["conv_p"] = qkv_p[:, L - keep:]
            outs["conv_s"] = jnp.concatenate([state_gdn_conv, qkv_s], axis=1)[:, -keep:]
            outs["S_p"] = S_p
            outs["S_s"] = S_s

        x, xb = _layer_norm(y, ln1_g[i], ln1_b[i])
        (h,) = _matmul(xb, w_mlp_up[i], epilogue=_epi_relu2, out_dtypes=[BF16], name="mlp_up")
        y = _mm_residual(h, w_mlp_down[i], x, alpha, "mlp_down")
        x, xb = _layer_norm(y, ln2_g[i], ln2_b[i])
        pcat = jnp.concatenate([p_prompt[i].reshape(Mp, -1), p_sample[i].reshape(Ms, -1)], axis=0)
        pdim = pcat.shape[1]
        tm, tn, _ = _mm_tiles(M, D, D, tn_target=256)
        x, xb = _matmul(
            xb, w_ple_gate[i], epilogue=_epi_ple, out_dtypes=[F32, BF16], tn_target=256, name="ple",
            extras=[_tile_extra(x, tm, tn),
                    (pcat, (tm, pdim), lambda i_, j, k: (i_, 0)),
                    (w_ple_proj[i], (pdim, tn), lambda i_, j, k: (0, j))])

    return (x[:Mp].reshape(B, L, D), x[Mp:].reshape(Bs, Ls, D),
            outs["k_diff_p"], outs["v_diff_p"], outs["k_diff_s"], outs["v_diff_s"],
            outs["k_fox_p"], outs["v_fox_p"], outs["logf_fox_p"],
            outs["k_fox_s"], outs["v_fox_s"], outs["logf_fox_s"],
            outs["pool_p"], outs["pool_s"], outs["conv_p"], outs["S_p"],
            outs["conv_s"], outs["S_s"])
```

```python
import functools
import math

import jax
import jax.numpy as jnp
from jax import lax
from jax.experimental import pallas as pl
from jax.experimental.pallas import tpu as pltpu

F32 = jnp.float32
BF16 = jnp.bfloat16

HEAD_DIM = 128
POOL_WINDOWS = (2, 4, 8, 16)
POOL_HALO = 16
CONV_HALO = 8
FLASH_COMPONENTS = 4
GDN_CHUNK = 64
GDN_HEAD_GROUP = 16
GDN_STACK = 4
GDN_HI_ROUNDS = 2
ROPE_THETA = 10000.0
LN_EPS = 1e-5
RMS_EPS = 1e-5
N_MIXERS = 4
LANES = 128
SUBLANES = 8
VMEM_LIMIT = 56 * 1024 * 1024

_NT = (((1,), (1,)), ((), ()))
_TN = (((0,), (0,)), ((), ()))


def _params(*sem):
    return pltpu.CompilerParams(dimension_semantics=sem, vmem_limit_bytes=VMEM_LIMIT)


def _pick(dim, target, align):
    best = None
    for t in range(align, min(dim, target) + 1, align):
        if dim % t == 0:
            best = t
    return best if best is not None else dim


def _sigmoid(x):
    return 1.0 / (1.0 + jnp.exp(-x))


def _softplus(x):
    return jnp.maximum(x, 0.0) + jnp.log1p(jnp.exp(-jnp.abs(x)))


def _split3(a):
    hi = a.astype(BF16)
    r = a - hi.astype(F32)
    mid = r.astype(BF16)
    lo = (r - mid.astype(F32)).astype(BF16)
    return hi, mid, lo


def _split2(a):
    hi = a.astype(BF16)
    return hi, (a - hi.astype(F32)).astype(BF16)


def _dot(a, b):
    return jnp.dot(a, b, preferred_element_type=F32)


def _dot_hi(a, b):
    a1, a2 = _split2(a)
    b1, b2 = _split2(b)
    return _dot(a1, b1) + (_dot(a1, b2) + _dot(a2, b1))


def _dot_lo(a, b):
    return _dot(a.astype(BF16), b.astype(BF16))


def _tri_dot_hi(tri_bf16, x):
    x1, x2, x3 = _split3(x)
    return _dot(tri_bf16, x1) + _dot(tri_bf16, x2) + _dot(tri_bf16, x3)


def _mm_body(*refs, nk, n_extra, n_out, epilogue):
    x_ref, w_ref = refs[0], refs[1]
    extra = refs[2:2 + n_extra]
    outs = refs[2 + n_extra:2 + n_extra + n_out]
    if nk == 1:
        epilogue(_dot(x_ref[...].astype(BF16), w_ref[...].astype(BF16)), extra, outs)
        return
    acc_ref = refs[-1]
    k = pl.program_id(2)

    @pl.when(k == 0)
    def _():
        acc_ref[...] = jnp.zeros_like(acc_ref)

    acc_ref[...] += _dot(x_ref[...].astype(BF16), w_ref[...].astype(BF16))

    @pl.when(k == nk - 1)
    def _():
        epilogue(acc_ref[...], extra, outs)


def _mm_tiles(M, K, n_cols, tm_target=1376, tn_target=512, tk_target=None):
    if tk_target is None:
        tk_target = K if K <= 4096 else 2048
    return _pick(M, tm_target, 16), _pick(n_cols, tn_target, LANES), _pick(K, tk_target, LANES)


def _matmul(x, w, *, epilogue, out_dtypes, col_off=0, n_cols=None, extras=(), out_lane=None,
            layer=None, name="mm", **tile_kw):
    M, K = x.shape
    n_cols = w.shape[-1] - col_off if n_cols is None else n_cols
    tm, tn, tk = _mm_tiles(M, K, n_cols, **tile_kw)
    assert col_off % tn == 0
    joff = col_off // tn
    nk = K // tk
    grid = (M // tm, n_cols // tn, nk)
    xspec = (pl.BlockSpec((tm, tk), lambda i, j, k: (i, k), pipeline_mode=pl.Buffered(1))
             if nk == 1 else pl.BlockSpec((tm, tk), lambda i, j, k: (i, k)))
    if layer is None:
        wspec = pl.BlockSpec((tk, tn), lambda i, j, k: (k, j + joff))
    else:
        wspec = pl.BlockSpec((None, tk, tn), lambda i, j, k: (layer, k, j + joff))
    in_specs = [xspec, wspec]
    args = [x, w]
    for arr, bshape, imap in extras:
        in_specs.append(pl.BlockSpec(bshape, imap))
        args.append(arr)
    out_lane = [None] * len(out_dtypes) if out_lane is None else out_lane
    out_shape, out_specs = [], []
    for dt, dh in zip(out_dtypes, out_lane):
        if dh is None:
            out_shape.append(jax.ShapeDtypeStruct((M, n_cols), dt))
            out_specs.append(pl.BlockSpec((tm, tn), lambda i, j, k: (i, j)))
        else:
            out_shape.append(jax.ShapeDtypeStruct((M, n_cols // dh, dh), dt))
            out_specs.append(pl.BlockSpec((tm, tn // dh, dh), lambda i, j, k: (i, j, 0)))
    body = functools.partial(_mm_body, nk=nk, n_extra=len(extras), n_out=len(out_dtypes),
                             epilogue=epilogue)
    return pl.pallas_call(
        body, grid=grid, in_specs=in_specs, out_specs=out_specs, out_shape=out_shape,
        scratch_shapes=[pltpu.VMEM((tm, tn), F32)] if nk > 1 else [],
        compiler_params=_params("parallel", "parallel", "arbitrary"), name=name,
    )(*args)


def _put(o_ref, val):
    o_ref[...] = val.astype(o_ref.dtype).reshape(o_ref.shape)


def _epi_store(acc, extra, outs):
    for o in outs:
        _put(o, acc)


def _epi_rope(acc, extra, outs, *, scale):
    cos = extra[0][...]
    sin = extra[1][...]
    parts = []
    for c in range(acc.shape[1] // HEAD_DIM):
        xc = acc[:, c * HEAD_DIM:(c + 1) * HEAD_DIM]
        parts.append(xc * cos + pltpu.roll(xc, HEAD_DIM // 2, 1) * sin)
    r = jnp.concatenate(parts, axis=1)
    if scale != 1.0:
        r = r * scale
    for o in outs:
        _put(o, r)


def _epi_scale(acc, extra, outs, *, scale):
    for o in outs:
        _put(o, acc * scale)


def _epi_relu2(acc, extra, outs):
    h = jnp.maximum(acc, 0.0)
    _put(outs[0], h * h)


def _epi_residual(acc, extra, outs, *, alpha):
    _put(outs[0], alpha * extra[0][...] + acc)


def _epi_ple(acc, extra, outs):
    x_ref, p_ref, wp_ref = extra
    pp = _dot(p_ref[...].astype(BF16), wp_ref[...].astype(BF16))
    y = x_ref[...] + _sigmoid(acc) * pp
    for o in outs:
        _put(o, y)


def _epi_logsigmoid(acc, extra, outs):
    z = acc + extra[0][...]
    _put(outs[0], -(jnp.maximum(-z, 0.0) + jnp.log1p(jnp.exp(-jnp.abs(z)))))


def _tile_extra(arr, tm, tn):
    return (arr, (tm, tn), lambda i, j, k: (i, j))


def _mm_plain(x, w, out_dtypes, **kw):
    return _matmul(x, w, epilogue=_epi_store, out_dtypes=out_dtypes, **kw)


def _mm_residual(x, w, resid, alpha, name, layer=None, **tile_kw):
    M, N = resid.shape
    tm, tn, _ = _mm_tiles(M, x.shape[1], N, **tile_kw)
    return _matmul(x, w, epilogue=functools.partial(_epi_residual, alpha=alpha), out_dtypes=[F32],
                   extras=[_tile_extra(resid, tm, tn)], layer=layer, name=name, **tile_kw)[0]


def _ln_body(y_ref, g_ref, b_ref, o32_ref, o16_ref):
    y = y_ref[...]
    mu = jnp.mean(y, axis=-1, keepdims=True)
    d = y - mu
    var = jnp.mean(d * d, axis=-1, keepdims=True)
    out = d * lax.rsqrt(var + LN_EPS) * g_ref[...] + b_ref[...]
    o32_ref[...] = out
    o16_ref[...] = out.astype(BF16)


def _layer_norm(y, g, b):
    M, D = y.shape
    tm = _pick(M, 256, 16)
    row = pl.BlockSpec((tm, D), lambda i: (i, 0))
    vec = pl.BlockSpec((1, D), lambda i: (0, 0))
    return pl.pallas_call(
        _ln_body, grid=(M // tm,), in_specs=[row, vec, vec], out_specs=[row, row],
        out_shape=[jax.ShapeDtypeStruct((M, D), F32), jax.ShapeDtypeStruct((M, D), BF16)],
        compiler_params=_params("parallel"), name="layer_norm",
    )(y, g.reshape(1, D), b.reshape(1, D))


def _lambda_full(lam_ref, lambda_init):
    lam = lam_ref[...]
    e1 = jnp.exp(jnp.sum(lam[0:1] * lam[1:2], axis=-1, keepdims=True))
    e2 = jnp.exp(jnp.sum(lam[2:3] * lam[3:4], axis=-1, keepdims=True))
    return e1 - e2 + lambda_init


def _diff_finish(o0, o1, lam, subw, lambda_init):
    o = o0 - lam * o1
    o = o * lax.rsqrt(jnp.mean(o * o, axis=-1, keepdims=True) + RMS_EPS) * subw
    return o * (1.0 - lambda_init)


def _flash_body(*refs, mode, nk, tq, tk, nc, lambda_init):
    if mode == "diff":
        q_ref, k_ref, v_ref, lam_ref, subw_ref, o_ref, m_scr, l_scr, acc_scr = refs
    else:
        q_ref, k_ref, v_ref, cq_ref, ck_ref, o_ref, m_scr, l_scr, acc_scr = refs
    qi = pl.program_id(2)
    ki = pl.program_id(3)
    dv = acc_scr.shape[-1]

    @pl.when(ki == 0)
    def _():
        m_scr[...] = jnp.full(m_scr.shape, -jnp.inf, F32)
        l_scr[...] = jnp.zeros(l_scr.shape, F32)
        acc_scr[...] = jnp.zeros(acc_scr.shape, F32)

    def update(on_diagonal):
        q = q_ref[...]
        k = k_ref[...]
        v = v_ref[...]
        if on_diagonal:
            mask = (lax.broadcasted_iota(jnp.int32, (tq, tk), 1)
                    <= lax.broadcasted_iota(jnp.int32, (tq, tk), 0))
        for c in range(nc):
            sl = slice(c * HEAD_DIM, (c + 1) * HEAD_DIM)
            s = lax.dot_general(q[:, sl], k[:, sl], _NT, preferred_element_type=F32)
            if mode == "fox":
                s = s + (cq_ref[0][:, c:c + 1] - ck_ref[0][c:c + 1, :])
            if on_diagonal:
                s = jnp.where(mask, s, -jnp.inf)
            m_old = m_scr[c]
            m_new = jnp.maximum(m_old, jnp.max(s, axis=-1, keepdims=True))
            alpha = jnp.exp(m_old - m_new)
            p = jnp.exp(s - m_new)
            l_scr[c] = alpha * l_scr[c] + jnp.sum(p, axis=-1, keepdims=True)
            vv = v[:, (c // 2) * dv:(c // 2 + 1) * dv] if mode == "diff" else v[:, sl]
            acc_scr[c] = alpha * acc_scr[c] + _dot(p.astype(BF16), vv)
            m_scr[c] = m_new

    @pl.when(ki < qi)
    def _():
        update(False)

    @pl.when(ki == qi)
    def _():
        update(True)

    @pl.when(ki == nk - 1)
    def _():
        o = [acc_scr[c] / l_scr[c] for c in range(nc)]
        if mode == "diff":
            lam = _lambda_full(lam_ref, lambda_init)
            o = [_diff_finish(o[c], o[c + 1], lam, subw_ref[...], lambda_init)
                 for c in range(0, nc, 2)]
        o_ref[...] = jnp.concatenate(o, axis=1).astype(o_ref.dtype)


def _flash_prompt(q, k, v, *, B, L, mode, extra, lambda_init=0.0):
    D = q.shape[1]
    nc = FLASH_COMPONENTS
    W = nc * HEAD_DIM
    assert D % W == 0
    tq = tk = _pick(L, 512, LANES)
    nq = L // tq
    grid = (B, D // W, nq, nq)
    qspec = pl.BlockSpec((tq, W), lambda b, h, qi, ki: (b * nq + qi, h))
    kspec = pl.BlockSpec((tk, W), lambda b, h, qi, ki: (b * nq + jnp.minimum(ki, qi), h))
    if mode == "diff":
        lam4, subw = extra
        especs = [pl.BlockSpec(lam4.shape, lambda b, h, qi, ki: (0, 0)),
                  pl.BlockSpec(subw.shape, lambda b, h, qi, ki: (0, 0))]
        dv = 2 * HEAD_DIM
    else:
        c_rows, c_cols = extra
        extra = (c_rows.reshape(B * L, D // W, nc).transpose(1, 0, 2),
                 c_cols.reshape(D // W, nc, B * L))
        especs = [pl.BlockSpec((1, tq, nc), lambda b, h, qi, ki: (h, b * nq + qi, 0)),
                  pl.BlockSpec((1, nc, tk), lambda b, h, qi, ki: (h, 0, b * nq + jnp.minimum(ki, qi)))]
        dv = HEAD_DIM
    body = functools.partial(_flash_body, mode=mode, nk=nq, tq=tq, tk=tk, nc=nc,
                             lambda_init=lambda_init)
    return pl.pallas_call(
        body, grid=grid, in_specs=[qspec, kspec, kspec] + especs,
        out_specs=pl.BlockSpec((tq, W), lambda b, h, qi, ki: (b * nq + qi, h)),
        out_shape=jax.ShapeDtypeStruct((B * L, D), BF16),
        scratch_shapes=[pltpu.VMEM((nc, tq, 1), F32), pltpu.VMEM((nc, tq, 1), F32),
                        pltpu.VMEM((nc, tq, dv), F32)],
        compiler_params=_params("parallel", "parallel", "parallel", "arbitrary"),
        name="flash_" + mode,
    )(q, k, v, *extra)


def _decode_body(*refs, mode, npages, nh, ls, lambda_init):
    pt_ref = refs[0]
    if mode == "diff":
        (q_ref, kc_ref, vc_ref, kn_ref, vn_ref, lam_ref, subw_ref,
         o_ref, m_scr, l_scr, acc_scr) = refs[1:]
    else:
        (q_ref, kc_ref, vc_ref, kn_ref, vn_ref, cq_ref, ckc_ref, ckn_ref,
         o_ref, m_scr, l_scr, acc_scr) = refs[1:]
    del pt_ref
    p = pl.program_id(1)
    P = kc_ref.shape[1]
    dv = acc_scr.shape[-1]

    @pl.when(p == 0)
    def _():
        m_scr[...] = jnp.full(m_scr.shape, -jnp.inf, F32)
        l_scr[...] = jnp.zeros(l_scr.shape, F32)
        acc_scr[...] = jnp.zeros(acc_scr.shape, F32)

    def step(k_of, v_of, ck, mask):
        q = q_ref[0]
        probs = []
        for h in range(nh):
            s = lax.dot_general(q[:, h * HEAD_DIM:(h + 1) * HEAD_DIM], k_of(h), _NT,
                                preferred_element_type=F32)
            if mode == "fox":
                s = s + (cq_ref[0][:, h:h + 1] - ck[h:h + 1, :])
            if mask is not None:
                s = jnp.where(mask, s, -jnp.inf)
            m_old = m_scr[h]
            m_new = jnp.maximum(m_old, jnp.max(s, axis=-1, keepdims=True))
            alpha = jnp.exp(m_old - m_new)
            pr = jnp.exp(s - m_new)
            l_scr[h] = alpha * l_scr[h] + jnp.sum(pr, axis=-1, keepdims=True)
            m_scr[h] = m_new
            probs.append((alpha, pr.astype(BF16)))
        if mode == "diff":
            for hh in range(nh // 2):
                (a0, p0), (a1, p1) = probs[2 * hh], probs[2 * hh + 1]
                pv = _dot(jnp.concatenate([p0, p1], axis=0), v_of(hh))
                al = jnp.concatenate([a0, a1], axis=0)
                acc_scr[hh] = al * acc_scr[hh] + pv
        else:
            for h in range(nh):
                a0, p0 = probs[h]
                acc_scr[h] = a0 * acc_scr[h] + _dot(p0, v_of(h))

    @pl.when(p < npages)
    def _():
        kc = pltpu.einshape("khd->hkd", kc_ref[0].astype(BF16))
        vc = pltpu.einshape("khd->hkd", vc_ref[0].astype(BF16))
        ck = ckc_ref[0] if mode == "fox" else None
        step(lambda h: kc[h], lambda h: vc[h], ck, None)

    @pl.when(p == npages)
    def _():
        row = lax.broadcasted_iota(jnp.int32, (ls, P), 0)
        col = lax.broadcasted_iota(jnp.int32, (ls, P), 1)
        ck = ckn_ref[0] if mode == "fox" else None
        kn = kn_ref[0]
        vn = vn_ref[0]
        step(lambda h: kn[:, h * HEAD_DIM:(h + 1) * HEAD_DIM],
             lambda h: vn[:, h * dv:(h + 1) * dv], ck, col <= row)
        outs = []
        if mode == "diff":
            lam = _lambda_full(lam_ref, lambda_init)
            for hh in range(nh // 2):
                acc = acc_scr[hh]
                o0 = acc[:ls] / l_scr[2 * hh]
                o1 = acc[ls:] / l_scr[2 * hh + 1]
                outs.append(_diff_finish(o0, o1, lam, subw_ref[...], lambda_init))
        else:
            for h in range(nh):
                outs.append(acc_scr[h] / l_scr[h])
        o_ref[0] = jnp.concatenate(outs, axis=1).astype(o_ref.dtype)


def _decode_attention(q_s, k_new, v_new, cache_k, cache_v, page_table, *, mode, extra,
                      lambda_init=0.0):
    Bs, Ls, D = q_s.shape
    P = cache_k.shape[1]
    npages = page_table.shape[1]
    nh = D // HEAD_DIM
    pad = ((0, 0), (0, P - Ls), (0, 0))
    kn = jnp.pad(k_new, pad)
    vn = jnp.pad(v_new, pad)
    page = lambda b, p, pt: (pt[b, jnp.minimum(p, npages - 1)], 0, 0, 0)
    per_b = lambda b, p, pt: (b, 0, 0)
    in_specs = [pl.BlockSpec((1, Ls, D), per_b),
                pl.BlockSpec((1,) + cache_k.shape[1:], page),
                pl.BlockSpec((1,) + cache_v.shape[1:], page),
                pl.BlockSpec((1, P, D), per_b), pl.BlockSpec((1, P, D), per_b)]
    if mode == "diff":
        lam4, subw = extra
        in_specs += [pl.BlockSpec(lam4.shape, lambda b, p, pt: (0, 0)),
                     pl.BlockSpec(subw.shape, lambda b, p, pt: (0, 0))]
        acc_shape = (nh // 2, 2 * Ls, 2 * HEAD_DIM)
    else:
        cq, ckc, ckn = extra
        in_specs += [pl.BlockSpec((1, Ls, nh), per_b),
                     pl.BlockSpec((1, nh, P), lambda b, p, pt: (b, 0, jnp.minimum(p, npages - 1))),
                     pl.BlockSpec((1, nh, P), per_b)]
        acc_shape = (nh, Ls, HEAD_DIM)
    body = functools.partial(_decode_body, mode=mode, npages=npages, nh=nh, ls=Ls,
                             lambda_init=lambda_init)
    grid_spec = pltpu.PrefetchScalarGridSpec(
        num_scalar_prefetch=1, grid=(Bs, npages + 1), in_specs=in_specs,
        out_specs=pl.BlockSpec((1, Ls, D), per_b),
        scratch_shapes=[pltpu.VMEM((nh, Ls, 1), F32), pltpu.VMEM((nh, Ls, 1), F32),
                        pltpu.VMEM(acc_shape, F32)])
    return pl.pallas_call(
        body, grid_spec=grid_spec, out_shape=jax.ShapeDtypeStruct((Bs, Ls, D), BF16),
        compiler_params=_params("parallel", "arbitrary"), name="decode_" + mode,
    )(page_table, q_s, cache_k, cache_v, kn, vn, *extra)


def _cumsum_body(tbl_ref, x_ref, init_ref, o_ref, carry, *, tb):
    del tbl_ref
    j = pl.program_id(1)

    @pl.when(j == 0)
    def _():
        carry[...] = init_ref[0]

    r = lax.broadcasted_iota(jnp.int32, (tb, tb), 0)
    c = lax.broadcasted_iota(jnp.int32, (tb, tb), 1)
    tri = (r >= c).astype(BF16)
    out = _tri_dot_hi(tri, x_ref[0]) + carry[...]
    o_ref[0] = out
    carry[...] = out[tb - 1:tb, :]


def _cumsum_blocks(src, table, init):
    _, tb, H = src.shape
    S, nb = table.shape
    grid_spec = pltpu.PrefetchScalarGridSpec(
        num_scalar_prefetch=1, grid=(S, nb),
        in_specs=[pl.BlockSpec((1, tb, H), lambda s, j, t: (t[s, j], 0, 0)),
                  pl.BlockSpec((1, 1, H), lambda s, j, t: (s, 0, 0))],
        out_specs=pl.BlockSpec((1, tb, H), lambda s, j, t: (s, j, 0)),
        scratch_shapes=[pltpu.VMEM((1, H), F32)])
    return pl.pallas_call(
        functools.partial(_cumsum_body, tb=tb), grid_spec=grid_spec,
        out_shape=jax.ShapeDtypeStruct((S, nb * tb, H), F32),
        compiler_params=_params("parallel", "arbitrary"), name="logf_cumsum",
    )(table, src, init)


def _pool_body(x_ref, prev_ref, w_ref, scale_ref, o_ref, carry, *, tl, start, alpha):
    g = pl.program_id(1)
    t = pl.program_id(2)

    @pl.when(t == 0)
    def _():
        carry[...] = prev_ref[0]

    x = x_ref[...]
    xe = jnp.concatenate([carry[...], x], axis=0)
    s2 = xe + pltpu.roll(xe, 1, 0)
    s4 = s2 + pltpu.roll(s2, 2, 0)
    s8 = s4 + pltpu.roll(s4, 4, 0)
    s16 = s8 + pltpu.roll(s8, 8, 0)
    win = jnp.where(g == 0, s2, jnp.where(g == 1, s4, jnp.where(g == 2, s8, s16)))[POOL_HALO:]
    width = jnp.left_shift(2, g)
    pos = start + t * tl + lax.broadcasted_iota(jnp.int32, (tl, 1), 0)
    cnt = jnp.minimum(pos + 1, width).astype(F32)
    mixed = win / cnt - x
    y = _dot(mixed.astype(BF16), w_ref[0].astype(BF16)) * scale_ref[...]
    o_ref[...] = alpha * x + y
    carry[...] = xe[tl:tl + POOL_HALO]


def _pool_mixer(x, prev, w_pool, pool_scale, *, row0, B, L, start, alpha):
    D = x.shape[1]
    ng = len(POOL_WINDOWS)
    G = D // ng
    tl = _pick(L, 512, SUBLANES)
    nt = L // tl
    r0 = row0 // tl
    assert row0 % tl == 0
    body = functools.partial(_pool_body, tl=tl, start=start, alpha=alpha)
    return pl.pallas_call(
        body, grid=(B, ng, nt),
        in_specs=[pl.BlockSpec((tl, G), lambda b, g, t: (r0 + b * nt + t, g)),
                  pl.BlockSpec((1, POOL_HALO, G), lambda b, g, t: (b, 0, g)),
                  pl.BlockSpec((1, G, G), lambda b, g, t: (g, 0, 0)),
                  pl.BlockSpec((1, G), lambda b, g, t: (0, g))],
        out_specs=pl.BlockSpec((tl, G), lambda b, g, t: (b * nt + t, g)),
        out_shape=jax.ShapeDtypeStruct((B * L, D), F32),
        scratch_shapes=[pltpu.VMEM((POOL_HALO, G), F32)],
        compiler_params=_params("parallel", "parallel", "arbitrary"), name="pool_mixer",
    )(x, prev, w_pool, pool_scale.reshape(1, D))


def _conv_body(x_ref, prev_ref, w_ref, o_ref, carry, *, tl, ncb_part, qscale):
    cb = pl.program_id(1)
    t = pl.program_id(2)

    @pl.when(t == 0)
    def _():
        carry[...] = prev_ref[0]

    x = x_ref[...]
    w = w_ref[...]
    xe = jnp.concatenate([carry[...], x], axis=0)
    x1 = pltpu.roll(xe, 1, 0)[CONV_HALO:]
    x2 = pltpu.roll(xe, 2, 0)[CONV_HALO:]
    x3 = pltpu.roll(xe, 3, 0)[CONV_HALO:]
    conv = x3 * w[0:1] + x2 * w[1:2] + x1 * w[2:3] + x * w[3:4]
    act = conv * _sigmoid(conv)
    part = cb // ncb_part
    scale = jnp.where(part == 0, qscale, 1.0)
    pieces = []
    for c in range(act.shape[1] // HEAD_DIM):
        a = act[:, c * HEAD_DIM:(c + 1) * HEAD_DIM]
        n = a * lax.rsqrt(jnp.sum(a * a, axis=-1, keepdims=True) + 1e-6) * scale
        pieces.append(jnp.where(part < 2, n, a))
    o_ref[...] = jnp.concatenate(pieces, axis=1)
    carry[...] = xe[tl:tl + CONV_HALO]


def _gdn_conv(qkv, prev, conv_w, *, row0, B, L, D):
    assert conv_w.shape[0] == 4
    C3 = qkv.shape[1]
    tc = _pick(D, 512, LANES)
    tl = _pick(L, 512, SUBLANES)
    nt = L // tl
    r0 = row0 // tl
    assert row0 % tl == 0
    body = functools.partial(_conv_body, tl=tl, ncb_part=D // tc, qscale=HEAD_DIM ** -0.5)
    return pl.pallas_call(
        body, grid=(B, C3 // tc, nt),
        in_specs=[pl.BlockSpec((tl, tc), lambda b, c, t: (r0 + b * nt + t, c)),
                  pl.BlockSpec((1, CONV_HALO, tc), lambda b, c, t: (b, 0, c)),
                  pl.BlockSpec((4, tc), lambda b, c, t: (0, c))],
        out_specs=pl.BlockSpec((tl, tc), lambda b, c, t: (b * nt + t, c)),
        out_shape=jax.ShapeDtypeStruct((B * L, C3), F32),
        scratch_shapes=[pltpu.VMEM((CONV_HALO, tc), F32)],
        compiler_params=_params("parallel", "parallel", "arbitrary"), name="gdn_conv",
    )(qkv, prev, conv_w)


def _gdn_body(q_ref, k_ref, v_ref, z_ref, ab_ref, alog_ref, dtb_ref, nw_ref, s0_ref,
              o_ref, s_ref, gct_scr, *, C, HG, G4, nh, valid_len):
    hg = pl.program_id(1)
    c = pl.program_id(2)

    @pl.when(c == 0)
    def _():
        s_ref[...] = s0_ref[...]

    ab = ab_ref[...]
    row = c * C + lax.broadcasted_iota(jnp.int32, (C, 1), 0)
    valid = row < valid_len
    beta = jnp.where(valid, _sigmoid(ab[:, :nh]), 0.0)
    g = jnp.where(valid, -jnp.exp(alog_ref[...]) * _softplus(ab[:, nh:] + dtb_ref[...]), 0.0)
    ii = lax.broadcasted_iota(jnp.int32, (C, C), 0)
    jj = lax.broadcasted_iota(jnp.int32, (C, C), 1)
    gc = _tri_dot_hi((ii >= jj).astype(BF16), g)
    g_last = gc[C - 1:C, :]
    e_gc = jnp.exp(gc)
    e_kd = jnp.exp(g_last - gc)
    e_end = jnp.exp(g_last)
    gpad = jnp.concatenate([gc, jnp.zeros((C, LANES - nh), F32)], axis=1)
    gpad = jnp.concatenate([gpad, jnp.zeros((LANES - C, LANES), F32)], axis=0)
    gct_scr[...] = gpad.T
    lane_h = lax.broadcasted_iota(jnp.int32, (1, nh), 1)
    nw = nw_ref[...]
    rounds = max(int(math.ceil(math.log2(C))) - 1, 0)

    W4 = G4 * C
    ri = lax.broadcasted_iota(jnp.int32, (W4, W4), 0)
    ci = lax.broadcasted_iota(jnp.int32, (W4, W4), 1)
    same = (ri // C) == (ci // C)
    incl = same & (ri >= ci)
    strict = same & (ri > ci)
    eye = (ri == ci).astype(F32)
    rows = lambda a, t: a[t * C:(t + 1) * C]

    for grp in range(HG // G4):
        cols, grow, eend = [], [], []
        for t in range(G4):
            head = hg * HG + grp * G4 + t
            sel = lane_h == head
            col = lambda a: jnp.sum(jnp.where(sel, a, 0.0), axis=1, keepdims=True)
            cols.append((col(gc), col(beta), col(e_gc), col(e_kd)))
            eend.append(col(e_end))
            grow.append(gct_scr[pl.ds(head, 1), :][:, :C])
        stack = lambda idx: jnp.concatenate([cl[idx] for cl in cols], axis=0)
        gcol, bcol, egc, ekd = stack(0), stack(1), stack(2), stack(3)
        grow = jnp.concatenate(grow, axis=1)
        sls = [slice((grp * G4 + t) * HEAD_DIM, (grp * G4 + t + 1) * HEAD_DIM) for t in range(G4)]
        q = jnp.concatenate([q_ref[:, sl] for sl in sls], axis=0)
        k = jnp.concatenate([k_ref[:, sl] for sl in sls], axis=0)
        v = jnp.concatenate([v_ref[:, sl] for sl in sls], axis=0)
        kb = k.astype(BF16)
        decay = jnp.exp(jnp.where(incl, gcol - grow, -jnp.inf))
        kk = lax.dot_general(kb, kb, _NT, preferred_element_type=F32)
        aqk = lax.dot_general(q.astype(BF16), kb, _NT, preferred_element_type=F32) * decay
        A = jnp.where(strict, bcol * kk * decay, 0.0)
        X = eye - A
        Pw = A
        for rd in range(rounds):
            mm = _dot_hi if rd < GDN_HI_ROUNDS else _dot_lo
            Pw = mm(Pw, Pw)
            X = X + mm(X, Pw)
        rhs = jnp.concatenate([bcol * v, (bcol * egc) * k], axis=1)
        sol = _dot_hi(X, rhs)
        u = sol[:, :HEAD_DIM]
        wkqg = jnp.concatenate([sol[:, HEAD_DIM:], q * egc], axis=1).astype(BF16)
        kd = (k * ekd).astype(BF16)
        S = [s_ref[0, grp * G4 + t] for t in range(G4)]
        w_parts, o_parts = [], []
        for t in range(G4):
            Sb = S[t].astype(BF16)
            both = _dot(jnp.concatenate([rows(wkqg[:, :HEAD_DIM], t), rows(wkqg[:, HEAD_DIM:], t)],
                                        axis=0), Sb)
            w_parts.append(rows(u, t) - both[:C])
            o_parts.append(both[C:])
        w = jnp.concatenate(w_parts, axis=0)
        wb = w.astype(BF16)
        o = jnp.concatenate(o_parts, axis=0) + _dot(aqk.astype(BF16), wb)
        for t in range(G4):
            s_ref[0, grp * G4 + t] = eend[t] * S[t] + lax.dot_general(
                rows(kd, t), rows(wb, t), _TN, preferred_element_type=F32)
            ot = rows(o, t)
            zz = z_ref[:, sls[t]]
            on = ot * lax.rsqrt(jnp.mean(ot * ot, axis=-1, keepdims=True) + RMS_EPS) * nw
            o_ref[:, sls[t]] = (on * (zz * _sigmoid(zz))).astype(o_ref.dtype)


def _gdn_core(qkvn, z, ab, S0, A_log, dt_bias, norm_w, *, B, L, valid_len, zrow0):
    D = qkvn.shape[1] // 3
    nh = D // HEAD_DIM
    C = GDN_CHUNK
    HG = min(GDN_HEAD_GROUP, nh)
    G4 = min(GDN_STACK, HG)
    assert HG % G4 == 0 and nh % HG == 0
    W = HG * HEAD_DIM
    nc = L // C
    npart = D // W
    zr0 = zrow0 // C
    assert L % C == 0 and zrow0 % C == 0
    body = functools.partial(_gdn_body, C=C, HG=HG, G4=G4, nh=nh, valid_len=valid_len)
    vec = lambda n: pl.BlockSpec((1, n), lambda b, h, c: (0, 0))
    return pl.pallas_call(
        body, grid=(B, nh // HG, nc),
        in_specs=[pl.BlockSpec((C, W), lambda b, h, c: (b * nc + c, h)),
                  pl.BlockSpec((C, W), lambda b, h, c: (b * nc + c, npart + h)),
                  pl.BlockSpec((C, W), lambda b, h, c: (b * nc + c, 2 * npart + h)),
                  pl.BlockSpec((C, W), lambda b, h, c: (zr0 + b * nc + c, h)),
                  pl.BlockSpec((C, 2 * nh), lambda b, h, c: (zr0 + b * nc + c, 0)),
                  vec(nh), vec(nh), vec(HEAD_DIM),
                  pl.BlockSpec((1, HG, HEAD_DIM, HEAD_DIM), lambda b, h, c: (b, h, 0, 0))],
        out_specs=[pl.BlockSpec((C, W), lambda b, h, c: (b * nc + c, h)),
                   pl.BlockSpec((1, HG, HEAD_DIM, HEAD_DIM), lambda b, h, c: (b, h, 0, 0))],
        out_shape=[jax.ShapeDtypeStruct((B * L, D), BF16),
                   jax.ShapeDtypeStruct((B, nh, HEAD_DIM, HEAD_DIM), F32)],
        scratch_shapes=[pltpu.VMEM((LANES, LANES), F32)],
        compiler_params=_params("parallel", "parallel", "arbitrary"), name="gdn_core",
    )(qkvn, qkvn, qkvn, z, ab, A_log.reshape(1, nh), dt_bias.reshape(1, nh),
      norm_w.reshape(1, HEAD_DIM), S0)


def _rope_tables(B, L, Bs, Ls, past_len):
    pos = jnp.concatenate([jnp.tile(jnp.arange(L), B), jnp.tile(past_len + jnp.arange(Ls), Bs)])
    inv = ROPE_THETA ** (-jnp.arange(0, HEAD_DIM, 2, dtype=F32) / HEAD_DIM)
    ang = pos.astype(F32)[:, None] * inv[None, :]
    cos = jnp.cos(ang)
    sin = jnp.sin(ang)
    return jnp.concatenate([cos, cos], axis=1), jnp.concatenate([-sin, sin], axis=1)


def _kv_proj(xb, w, col_off, D, dh, epilogue, extras, name):
    tn = SUBLANES * dh
    tm_target = 1376 if tn <= 1024 else 688
    return _matmul(xb, w, col_off=col_off, n_cols=D, epilogue=epilogue, extras=extras,
                   out_dtypes=[F32, BF16], out_lane=[dh, None], name=name,
                   tm_target=tm_target, tn_target=tn, tk_target=1024)


def kernel(x_prompt, x_sample, cache_k_diff, cache_v_diff, cache_k_fox, cache_v_fox, cache_logf_fox, state_pool, state_gdn_conv, state_gdn_S, page_table, p_prompt, p_sample, ln1_g, ln1_b, ln2_g, ln2_b, w_mlp_up, w_mlp_down, w_ple_gate, w_ple_proj, w_qkv_diff, lam_q1, lam_k1, lam_q2, lam_k2, subln_diff, w_o_diff, w_in_fox, b_f_fox, w_o_fox, w_pool, pool_scale, w_in_gdn, conv_gdn, A_log_gdn, dt_bias_gdn, norm_gdn, w_o_gdn):
    B, L, D = x_prompt.shape
    Bs, Ls, _ = x_sample.shape
    depth = ln1_g.shape[0]
    Mp, Ms = B * L, Bs * Ls
    M = Mp + Ms
    nh = D // HEAD_DIM
    P = cache_k_diff.shape[1]
    npages = page_table.shape[1]
    past_len = npages * P
    alpha = (2 * depth) ** 0.25
    qk_scale = HEAD_DIM ** -0.5

    x = jnp.concatenate([x_prompt.reshape(Mp, D), x_sample.reshape(Ms, D)], axis=0)
    xb = x.astype(BF16)
    outs = {}

    def rope_extras(tm):
        cos, sin = _rope_tables(B, L, Bs, Ls, past_len)
        return [(cos, (tm, HEAD_DIM), lambda i_, j, k: (i_, 0)),
                (sin, (tm, HEAD_DIM), lambda i_, j, k: (i_, 0))]

    for i in range(depth):
        kind = i % N_MIXERS
        if kind == 0:
            lambda_init = 0.8 - 0.6 * math.exp(-0.3 * i)
            dv = 2 * HEAD_DIM
            (q,) = _matmul(xb, w_qkv_diff, col_off=0, n_cols=D, out_dtypes=[BF16],
                           epilogue=functools.partial(_epi_rope, scale=qk_scale),
                           extras=rope_extras(_mm_tiles(M, D, D)[0]), name="diff_q")
            tmk = _mm_tiles(M, D, D, tn_target=SUBLANES * HEAD_DIM, tk_target=1024)[0]
            k4, kb = _kv_proj(xb, w_qkv_diff, D, D, HEAD_DIM,
                              functools.partial(_epi_rope, scale=1.0), rope_extras(tmk), "diff_k")
            v4, vb = _kv_proj(xb, w_qkv_diff, 2 * D, D, dv, _epi_store, (), "diff_v")
            lam4 = jnp.stack([lam_q1, lam_k1, lam_q2, lam_k2]).astype(F32)
            extra = (lam4, subln_diff.reshape(1, dv))
            o_p = _flash_prompt(q, kb, vb, B=B, L=L, mode="diff", extra=extra,
                                lambda_init=lambda_init)
            o_s = _decode_attention(q[Mp:].reshape(Bs, Ls, D), kb[Mp:].reshape(Bs, Ls, D),
                                    vb[Mp:].reshape(Bs, Ls, D), cache_k_diff, cache_v_diff,
                                    page_table, mode="diff", extra=extra, lambda_init=lambda_init)
            o = jnp.concatenate([o_p, o_s.reshape(Ms, D)], axis=0)
            y = _mm_residual(o, w_o_diff, x, alpha, "diff_out")
            outs["k_diff_p"] = k4[:Mp].reshape(B, L, nh, HEAD_DIM)
            outs["v_diff_p"] = v4[:Mp].reshape(B, L, D // dv, dv)
            outs["k_diff_s"] = k4[Mp:].reshape(Bs, Ls, nh, HEAD_DIM)
            outs["v_diff_s"] = v4[Mp:].reshape(Bs, Ls, D // dv, dv)
        elif kind == 1:
            (q,) = _matmul(xb, w_in_fox, col_off=0, n_cols=D, out_dtypes=[BF16],
                           epilogue=functools.partial(_epi_scale, scale=qk_scale), name="fox_q")
            k4, kb = _kv_proj(xb, w_in_fox, D, D, HEAD_DIM, _epi_store, (), "fox_k")
            v4, vb = _kv_proj(xb, w_in_fox, 2 * D, D, HEAD_DIM, _epi_store, (), "fox_v")
            (logf,) = _matmul(xb, w_in_fox[:, 3 * D:], epilogue=_epi_logsigmoid, out_dtypes=[F32],
                              extras=[(b_f_fox.reshape(1, nh), (1, nh), lambda i_, j, k: (0, 0))],
                              name="fox_logf")
            tb = P
            tbl_p = jnp.arange(Mp // tb, dtype=jnp.int32).reshape(B, L // tb)
            c_p = _cumsum_blocks(logf[:Mp].reshape(Mp // tb, tb, nh), tbl_p,
                                 jnp.zeros((B, 1, nh), F32)).reshape(Mp, nh)
            c_c = _cumsum_blocks(cache_logf_fox, page_table, jnp.zeros((Bs, 1, nh), F32))
            logf_s = logf[Mp:].reshape(Bs, Ls, nh)
            c_n = _cumsum_blocks(jnp.pad(logf_s, ((0, 0), (0, tb - Ls), (0, 0))),
                                 jnp.arange(Bs, dtype=jnp.int32).reshape(Bs, 1),
                                 c_c[:, past_len - 1:past_len, :])
            o_p = _flash_prompt(q, kb, vb, B=B, L=L, mode="fox", extra=(c_p, c_p.T))
            extra = (c_n[:, :Ls, :], c_c.transpose(0, 2, 1), c_n.transpose(0, 2, 1))
            o_s = _decode_attention(q[Mp:].reshape(Bs, Ls, D), kb[Mp:].reshape(Bs, Ls, D),
                                    vb[Mp:].reshape(Bs, Ls, D), cache_k_fox, cache_v_fox,
                                    page_table, mode="fox", extra=extra)
            o = jnp.concatenate([o_p, o_s.reshape(Ms, D)], axis=0)
            y = _mm_residual(o, w_o_fox, x, alpha, "fox_out")
            outs["k_fox_p"] = k4[:Mp].reshape(B, L, nh, HEAD_DIM)
            outs["v_fox_p"] = v4[:Mp].reshape(B, L, nh, HEAD_DIM)
            outs["logf_fox_p"] = logf[:Mp].reshape(B, L, nh)
            outs["k_fox_s"] = k4[Mp:].reshape(Bs, Ls, nh, HEAD_DIM)
            outs["v_fox_s"] = v4[Mp:].reshape(Bs, Ls, nh, HEAD_DIM)
            outs["logf_fox_s"] = logf_s
        elif kind == 2:
            keep = state_pool.shape[1]
            prev_p = jnp.zeros((B, POOL_HALO, D), F32)
            prev_s = jnp.pad(state_pool, ((0, 0), (POOL_HALO - keep, 0), (0, 0)))
            y_p = _pool_mixer(x, prev_p, w_pool, pool_scale, row0=0, B=B, L=L, start=0, alpha=alpha)
            y_s = _pool_mixer(x, prev_s, w_pool, pool_scale, row0=Mp, B=Bs, L=Ls, start=past_len,
                              alpha=alpha)
            y = jnp.concatenate([y_p, y_s], axis=0)
            xs3 = x[Mp:].reshape(Bs, Ls, D)
            outs["pool_p"] = jnp.stack([x[(b + 1) * L - keep:(b + 1) * L] for b in range(B)])
            outs["pool_s"] = jnp.concatenate([state_pool, xs3], axis=1)[:, -keep:]
        else:
            keep = state_gdn_conv.shape[1]
            (qkv,) = _mm_plain(xb, w_in_gdn, [F32], col_off=0, n_cols=3 * D, name="gdn_qkv")
            (z,) = _mm_plain(xb, w_in_gdn, [F32], col_off=3 * D, n_cols=D, name="gdn_z")
            (ab,) = _mm_plain(xb, w_in_gdn[:, 4 * D:], [F32], name="gdn_ab")
            prev_p = jnp.zeros((B, CONV_HALO, 3 * D), F32)
            prev_s = jnp.pad(state_gdn_conv, ((0, 0), (CONV_HALO - keep, 0), (0, 0)))
            cn_p = _gdn_conv(qkv, prev_p, conv_gdn, row0=0, B=B, L=L, D=D)
            cn_s = _gdn_conv(qkv, prev_s, conv_gdn, row0=Mp, B=Bs, L=Ls, D=D)
            C = GDN_CHUNK
            o_p, S_p = _gdn_core(cn_p, z, ab, jnp.zeros((B, nh, HEAD_DIM, HEAD_DIM), F32),
                                 A_log_gdn, dt_bias_gdn, norm_gdn, B=B, L=L, valid_len=L, zrow0=0)
            padrows = lambda a: jnp.pad(a.reshape(Bs, Ls, -1), ((0, 0), (0, C - Ls), (0, 0))
                                        ).reshape(Bs * C, -1)
            o_s, S_s = _gdn_core(padrows(cn_s), padrows(z[Mp:]), padrows(ab[Mp:]), state_gdn_S,
                                 A_log_gdn, dt_bias_gdn, norm_gdn, B=Bs, L=C, valid_len=Ls, zrow0=0)
            o = jnp.concatenate([o_p, o_s.reshape(Bs, C, D)[:, :Ls].reshape(Ms, D)], axis=0)
            y = _mm_residual(o, w_o_gdn, x, alpha, "gdn_out")
            qkv_s = qkv[Mp:].reshape(Bs, Ls, 3 * D)
            outs["conv_p"] = jnp.stack([qkv[(b + 1) * L - keep:(b + 1) * L] for b in range(B)])
            outs["conv_s"] = jnp.concatenate([state_gdn_conv, qkv_s], axis=1)[:, -keep:]
            outs["S_p"] = S_p
            outs["S_s"] = S_s

        x, xb = _layer_norm(y, ln1_g[i], ln1_b[i])
        (h,) = _matmul(xb, w_mlp_up, layer=i, epilogue=_epi_relu2, out_dtypes=[BF16], name="mlp_up")
        y = _mm_residual(h, w_mlp_down, x, alpha, "mlp_down", layer=i, tn_target=1024, tk_target=1024)
        x, xb = _layer_norm(y, ln2_g[i], ln2_b[i])
        pcat = jnp.concatenate([p_prompt[i].reshape(Mp, -1), p_sample[i].reshape(Ms, -1)], axis=0)
        pdim = pcat.shape[1]
        tm, tn, _ = _mm_tiles(M, D, D, tn_target=256)
        x, xb = _matmul(
            xb, w_ple_gate, layer=i, epilogue=_epi_ple, out_dtypes=[F32, BF16], tn_target=256,
            name="ple",
            extras=[_tile_extra(x, tm, tn),
                    (pcat, (tm, pdim), lambda i_, j, k: (i_, 0)),
                    (w_ple_proj, (None, pdim, tn), lambda i_, j, k, layer=i: (layer, 0, j))])

    return (x[:Mp].reshape(B, L, D), x[Mp:].reshape(Bs, Ls, D),
            outs["k_diff_p"], outs["v_diff_p"], outs["k_diff_s"], outs["v_diff_s"],
            outs["k_fox_p"], outs["v_fox_p"], outs["logf_fox_p"],
            outs["k_fox_s"], outs["v_fox_s"], outs["logf_fox_s"],
            outs["pool_p"], outs["pool_s"], outs["conv_p"], outs["S_p"],
            outs["conv_s"], outs["S_s"])
```

```python
import functools
import math

import jax
import jax.numpy as jnp
from jax import lax
from jax.experimental import pallas as pl
from jax.experimental.pallas import tpu as pltpu

F32 = jnp.float32
BF16 = jnp.bfloat16

HEAD_DIM = 128
POOL_WINDOWS = (2, 4, 8, 16)
POOL_HALO = 16
CONV_HALO = 8
FLASH_COMPONENTS = 4
GDN_CHUNK = 64
GDN_HEAD_GROUP = 16
GDN_STACK = 4
GDN_HI_ROUNDS = 2
ROPE_THETA = 10000.0
LN_EPS = 1e-5
RMS_EPS = 1e-5
N_MIXERS = 4
LANES = 128
SUBLANES = 8
VMEM_LIMIT = 56 * 1024 * 1024

_NT = (((1,), (1,)), ((), ()))
_TN = (((0,), (0,)), ((), ()))


def _params(*sem):
    return pltpu.CompilerParams(dimension_semantics=sem, vmem_limit_bytes=VMEM_LIMIT)


def _pick(dim, target, align):
    best = None
    for t in range(align, min(dim, target) + 1, align):
        if dim % t == 0:
            best = t
    return best if best is not None else dim


def _sigmoid(x):
    return 1.0 / (1.0 + jnp.exp(-x))


def _softplus(x):
    return jnp.maximum(x, 0.0) + jnp.log1p(jnp.exp(-jnp.abs(x)))


def _split3(a):
    hi = a.astype(BF16)
    r = a - hi.astype(F32)
    mid = r.astype(BF16)
    lo = (r - mid.astype(F32)).astype(BF16)
    return hi, mid, lo


def _split2(a):
    hi = a.astype(BF16)
    return hi, (a - hi.astype(F32)).astype(BF16)


def _dot(a, b):
    return jnp.dot(a, b, preferred_element_type=F32)


def _dot_hi(a, b):
    a1, a2 = _split2(a)
    b1, b2 = _split2(b)
    return _dot(a1, b1) + (_dot(a1, b2) + _dot(a2, b1))


def _dot_lo(a, b):
    return _dot(a.astype(BF16), b.astype(BF16))


def _tri_dot_hi(tri_bf16, x):
    x1, x2, x3 = _split3(x)
    return _dot(tri_bf16, x1) + _dot(tri_bf16, x2) + _dot(tri_bf16, x3)


def _mm_body(*refs, nk, n_extra, n_out, epilogue):
    x_ref, w_ref = refs[0], refs[1]
    extra = refs[2:2 + n_extra]
    outs = refs[2 + n_extra:2 + n_extra + n_out]
    if nk == 1:
        epilogue(_dot(x_ref[...].astype(BF16), w_ref[...].astype(BF16)), extra, outs)
        return
    acc_ref = refs[-1]
    k = pl.program_id(2)

    @pl.when(k == 0)
    def _():
        acc_ref[...] = jnp.zeros_like(acc_ref)

    acc_ref[...] += _dot(x_ref[...].astype(BF16), w_ref[...].astype(BF16))

    @pl.when(k == nk - 1)
    def _():
        epilogue(acc_ref[...], extra, outs)


def _mm_tiles(M, K, n_cols, tm_target=1376, tn_target=512, tk_target=None):
    if tk_target is None:
        tk_target = K if K <= 4096 else 2048
    return _pick(M, tm_target, 16), _pick(n_cols, tn_target, LANES), _pick(K, tk_target, LANES)


def _matmul(x, w, *, epilogue, out_dtypes, col_off=0, n_cols=None, extras=(), out_lane=None,
            layer=None, name="mm", **tile_kw):
    M, K = x.shape
    n_cols = w.shape[-1] - col_off if n_cols is None else n_cols
    tm, tn, tk = _mm_tiles(M, K, n_cols, **tile_kw)
    assert col_off % tn == 0
    joff = col_off // tn
    nk = K // tk
    grid = (M // tm, n_cols // tn, nk)
    xspec = (pl.BlockSpec((tm, tk), lambda i, j, k: (i, k), pipeline_mode=pl.Buffered(1))
             if nk == 1 else pl.BlockSpec((tm, tk), lambda i, j, k: (i, k)))
    if layer is None:
        wspec = pl.BlockSpec((tk, tn), lambda i, j, k: (k, j + joff))
    else:
        wspec = pl.BlockSpec((None, tk, tn), lambda i, j, k: (layer, k, j + joff))
    in_specs = [xspec, wspec]
    args = [x, w]
    for arr, bshape, imap in extras:
        in_specs.append(pl.BlockSpec(bshape, imap))
        args.append(arr)
    out_lane = [None] * len(out_dtypes) if out_lane is None else out_lane
    out_shape, out_specs = [], []
    for dt, dh in zip(out_dtypes, out_lane):
        if dh is None:
            out_shape.append(jax.ShapeDtypeStruct((M, n_cols), dt))
            out_specs.append(pl.BlockSpec((tm, tn), lambda i, j, k: (i, j)))
        else:
            out_shape.append(jax.ShapeDtypeStruct((M, n_cols // dh, dh), dt))
            out_specs.append(pl.BlockSpec((tm, tn // dh, dh), lambda i, j, k: (i, j, 0)))
    body = functools.partial(_mm_body, nk=nk, n_extra=len(extras), n_out=len(out_dtypes),
                             epilogue=epilogue)
    return pl.pallas_call(
        body, grid=grid, in_specs=in_specs, out_specs=out_specs, out_shape=out_shape,
        scratch_shapes=[pltpu.VMEM((tm, tn), F32)] if nk > 1 else [],
        compiler_params=_params("parallel", "parallel", "arbitrary"), name=name,
    )(*args)


def _put(o_ref, val):
    o_ref[...] = val.astype(o_ref.dtype).reshape(o_ref.shape)


def _epi_store(acc, extra, outs):
    for o in outs:
        _put(o, acc)


def _epi_rope(acc, extra, outs, *, scale):
    cos = extra[0][...]
    sin = extra[1][...]
    parts = []
    for c in range(acc.shape[1] // HEAD_DIM):
        xc = acc[:, c * HEAD_DIM:(c + 1) * HEAD_DIM]
        parts.append(xc * cos + pltpu.roll(xc, HEAD_DIM // 2, 1) * sin)
    r = jnp.concatenate(parts, axis=1)
    if scale != 1.0:
        r = r * scale
    for o in outs:
        _put(o, r)


def _epi_scale(acc, extra, outs, *, scale):
    for o in outs:
        _put(o, acc * scale)


def _epi_relu2(acc, extra, outs):
    h = jnp.maximum(acc, 0.0)
    _put(outs[0], h * h)


def _epi_residual(acc, extra, outs, *, alpha):
    _put(outs[0], alpha * extra[0][...] + acc)


def _epi_ple(acc, extra, outs):
    x_ref, p_ref, wp_ref = extra
    pp = _dot(p_ref[...].astype(BF16), wp_ref[...].astype(BF16))
    y = x_ref[...] + _sigmoid(acc) * pp
    for o in outs:
        _put(o, y)


def _epi_logsigmoid(acc, extra, outs):
    z = acc + extra[0][...]
    _put(outs[0], -(jnp.maximum(-z, 0.0) + jnp.log1p(jnp.exp(-jnp.abs(z)))))


def _tile_extra(arr, tm, tn):
    return (arr, (tm, tn), lambda i, j, k: (i, j))


def _mm_plain(x, w, out_dtypes, **kw):
    return _matmul(x, w, epilogue=_epi_store, out_dtypes=out_dtypes, **kw)


def _mm_residual(x, w, resid, alpha, name, layer=None, **tile_kw):
    M, N = resid.shape
    tm, tn, _ = _mm_tiles(M, x.shape[1], N, **tile_kw)
    return _matmul(x, w, epilogue=functools.partial(_epi_residual, alpha=alpha), out_dtypes=[F32],
                   extras=[_tile_extra(resid, tm, tn)], layer=layer, name=name, **tile_kw)[0]


def _ln_body(y_ref, g_ref, b_ref, o32_ref, o16_ref):
    y = y_ref[...]
    mu = jnp.mean(y, axis=-1, keepdims=True)
    d = y - mu
    var = jnp.mean(d * d, axis=-1, keepdims=True)
    out = d * lax.rsqrt(var + LN_EPS) * g_ref[...] + b_ref[...]
    o32_ref[...] = out
    o16_ref[...] = out.astype(BF16)


def _layer_norm(y, g, b):
    M, D = y.shape
    tm = _pick(M, 256, 16)
    row = pl.BlockSpec((tm, D), lambda i: (i, 0))
    vec = pl.BlockSpec((1, D), lambda i: (0, 0))
    return pl.pallas_call(
        _ln_body, grid=(M // tm,), in_specs=[row, vec, vec], out_specs=[row, row],
        out_shape=[jax.ShapeDtypeStruct((M, D), F32), jax.ShapeDtypeStruct((M, D), BF16)],
        compiler_params=_params("parallel"), name="layer_norm",
    )(y, g.reshape(1, D), b.reshape(1, D))


def _lambda_full(lam_ref, lambda_init):
    lam = lam_ref[...]
    e1 = jnp.exp(jnp.sum(lam[0:1] * lam[1:2], axis=-1, keepdims=True))
    e2 = jnp.exp(jnp.sum(lam[2:3] * lam[3:4], axis=-1, keepdims=True))
    return e1 - e2 + lambda_init


def _diff_finish(o0, o1, lam, subw, lambda_init):
    o = o0 - lam * o1
    o = o * lax.rsqrt(jnp.mean(o * o, axis=-1, keepdims=True) + RMS_EPS) * subw
    return o * (1.0 - lambda_init)


def _flash_body(*refs, mode, nk, tq, tk, nc, lambda_init):
    if mode == "diff":
        q_ref, k_ref, v_ref, lam_ref, subw_ref, o_ref, m_scr, l_scr, acc_scr = refs
    else:
        q_ref, k_ref, v_ref, cq_ref, ck_ref, o_ref, m_scr, l_scr, acc_scr = refs
    qi = pl.program_id(2)
    ki = pl.program_id(3)
    dv = acc_scr.shape[1]

    @pl.when(ki == 0)
    def _():
        m_scr[...] = jnp.full(m_scr.shape, -jnp.inf, F32)
        l_scr[...] = jnp.zeros(l_scr.shape, F32)
        acc_scr[...] = jnp.zeros(acc_scr.shape, F32)

    def update(on_diagonal):
        q = q_ref[...]
        k = k_ref[...]
        v = v_ref[...]
        if on_diagonal:
            mask = (lax.broadcasted_iota(jnp.int32, (tk, tq), 0)
                    <= lax.broadcasted_iota(jnp.int32, (tk, tq), 1))
        for c in range(nc):
            sl = slice(c * HEAD_DIM, (c + 1) * HEAD_DIM)
            s = lax.dot_general(k[:, sl], q[:, sl], _NT, preferred_element_type=F32)
            if mode == "fox":
                s = s + (cq_ref[0][c:c + 1, :] - ck_ref[0][:, c:c + 1])
            if on_diagonal:
                s = jnp.where(mask, s, -jnp.inf)
            m_old = m_scr[c]
            m_new = jnp.maximum(m_old, jnp.max(s, axis=0, keepdims=True))
            alpha = jnp.exp(m_old - m_new)
            p = jnp.exp(s - m_new)
            l_scr[c] = alpha * l_scr[c] + jnp.sum(p, axis=0, keepdims=True)
            vv = v[:, (c // 2) * dv:(c // 2 + 1) * dv] if mode == "diff" else v[:, sl]
            acc_scr[c] = alpha * acc_scr[c] + lax.dot_general(
                vv, p.astype(BF16), _TN, preferred_element_type=F32)
            m_scr[c] = m_new

    @pl.when(ki < qi)
    def _():
        update(False)

    @pl.when(ki == qi)
    def _():
        update(True)

    @pl.when(ki == nk - 1)
    def _():
        o = [acc_scr[c] / l_scr[c] for c in range(nc)]
        if mode == "diff":
            lam = _lambda_full(lam_ref, lambda_init)
            outs = []
            for c in range(0, nc, 2):
                d = o[c] - lam * o[c + 1]
                d = d * lax.rsqrt(jnp.mean(d * d, axis=0, keepdims=True) + RMS_EPS)
                outs.append(d.T * subw_ref[...] * (1.0 - lambda_init))
        else:
            outs = [oc.T for oc in o]
        o_ref[...] = jnp.concatenate(outs, axis=1).astype(o_ref.dtype)


def _flash_prompt(q, k, v, *, B, L, mode, extra, lambda_init=0.0):
    D = q.shape[1]
    nc = FLASH_COMPONENTS
    W = nc * HEAD_DIM
    assert D % W == 0
    tq = tk = _pick(L, 512, LANES)
    nq = L // tq
    grid = (B, D // W, nq, nq)
    qspec = pl.BlockSpec((tq, W), lambda b, h, qi, ki: (b * nq + qi, h))
    kspec = pl.BlockSpec((tk, W), lambda b, h, qi, ki: (b * nq + jnp.minimum(ki, qi), h))
    if mode == "diff":
        lam4, subw = extra
        especs = [pl.BlockSpec(lam4.shape, lambda b, h, qi, ki: (0, 0)),
                  pl.BlockSpec(subw.shape, lambda b, h, qi, ki: (0, 0))]
        dv = 2 * HEAD_DIM
    else:
        c_rows, c_cols = extra
        extra = (c_cols.reshape(D // W, nc, B * L),
                 c_rows.reshape(B * L, D // W, nc).transpose(1, 0, 2))
        especs = [pl.BlockSpec((1, nc, tq), lambda b, h, qi, ki: (h, 0, b * nq + qi)),
                  pl.BlockSpec((1, tk, nc), lambda b, h, qi, ki: (h, b * nq + jnp.minimum(ki, qi), 0))]
        dv = HEAD_DIM
    body = functools.partial(_flash_body, mode=mode, nk=nq, tq=tq, tk=tk, nc=nc,
                             lambda_init=lambda_init)
    return pl.pallas_call(
        body, grid=grid, in_specs=[qspec, kspec, kspec] + especs,
        out_specs=pl.BlockSpec((tq, W), lambda b, h, qi, ki: (b * nq + qi, h)),
        out_shape=jax.ShapeDtypeStruct((B * L, D), BF16),
        scratch_shapes=[pltpu.VMEM((nc, 1, tq), F32), pltpu.VMEM((nc, 1, tq), F32),
                        pltpu.VMEM((nc, dv, tq), F32)],
        compiler_params=_params("parallel", "parallel", "parallel", "arbitrary"),
        name="flash_" + mode,
    )(q, k, v, *extra)


def _decode_body(*refs, mode, npages, nh, ls, lambda_init):
    pt_ref = refs[0]
    if mode == "diff":
        (q_ref, kc_ref, vc_ref, kn_ref, vn_ref, lam_ref, subw_ref,
         o_ref, m_scr, l_scr, acc_scr) = refs[1:]
    else:
        (q_ref, kc_ref, vc_ref, kn_ref, vn_ref, cq_ref, ckc_ref, ckn_ref,
         o_ref, m_scr, l_scr, acc_scr) = refs[1:]
    del pt_ref
    p = pl.program_id(1)
    P = kc_ref.shape[1]
    dv = acc_scr.shape[-1]

    @pl.when(p == 0)
    def _():
        m_scr[...] = jnp.full(m_scr.shape, -jnp.inf, F32)
        l_scr[...] = jnp.zeros(l_scr.shape, F32)
        acc_scr[...] = jnp.zeros(acc_scr.shape, F32)

    def step(k_of, v_of, ck, mask):
        q = q_ref[0]
        probs = []
        for h in range(nh):
            s = lax.dot_general(q[:, h * HEAD_DIM:(h + 1) * HEAD_DIM], k_of(h), _NT,
                                preferred_element_type=F32)
            if mode == "fox":
                s = s + (cq_ref[0][:, h:h + 1] - ck[h:h + 1, :])
            if mask is not None:
                s = jnp.where(mask, s, -jnp.inf)
            m_old = m_scr[h]
            m_new = jnp.maximum(m_old, jnp.max(s, axis=-1, keepdims=True))
            alpha = jnp.exp(m_old - m_new)
            pr = jnp.exp(s - m_new)
            l_scr[h] = alpha * l_scr[h] + jnp.sum(pr, axis=-1, keepdims=True)
            m_scr[h] = m_new
            probs.append((alpha, pr.astype(BF16)))
        if mode == "diff":
            for hh in range(nh // 2):
                (a0, p0), (a1, p1) = probs[2 * hh], probs[2 * hh + 1]
                pv = _dot(jnp.concatenate([p0, p1], axis=0), v_of(hh))
                al = jnp.concatenate([a0, a1], axis=0)
                acc_scr[hh] = al * acc_scr[hh] + pv
        else:
            for h in range(nh):
                a0, p0 = probs[h]
                acc_scr[h] = a0 * acc_scr[h] + _dot(p0, v_of(h))

    @pl.when(p < npages)
    def _():
        kc = pltpu.einshape("khd->hkd", kc_ref[0].astype(BF16))
        vc = pltpu.einshape("khd->hkd", vc_ref[0].astype(BF16))
        ck = ckc_ref[0] if mode == "fox" else None
        step(lambda h: kc[h], lambda h: vc[h], ck, None)

    @pl.when(p == npages)
    def _():
        row = lax.broadcasted_iota(jnp.int32, (ls, P), 0)
        col = lax.broadcasted_iota(jnp.int32, (ls, P), 1)
        ck = ckn_ref[0] if mode == "fox" else None
        kn = kn_ref[0]
        vn = vn_ref[0]
        step(lambda h: kn[:, h * HEAD_DIM:(h + 1) * HEAD_DIM],
             lambda h: vn[:, h * dv:(h + 1) * dv], ck, col <= row)
        outs = []
        if mode == "diff":
            lam = _lambda_full(lam_ref, lambda_init)
            for hh in range(nh // 2):
                acc = acc_scr[hh]
                o0 = acc[:ls] / l_scr[2 * hh]
                o1 = acc[ls:] / l_scr[2 * hh + 1]
                outs.append(_diff_finish(o0, o1, lam, subw_ref[...], lambda_init))
        else:
            for h in range(nh):
                outs.append(acc_scr[h] / l_scr[h])
        o_ref[0] = jnp.concatenate(outs, axis=1).astype(o_ref.dtype)


def _decode_attention(q_s, k_new, v_new, cache_k, cache_v, page_table, *, mode, extra,
                      lambda_init=0.0):
    Bs, Ls, D = q_s.shape
    P = cache_k.shape[1]
    npages = page_table.shape[1]
    nh = D // HEAD_DIM
    pad = ((0, 0), (0, P - Ls), (0, 0))
    kn = jnp.pad(k_new, pad)
    vn = jnp.pad(v_new, pad)
    page = lambda b, p, pt: (pt[b, jnp.minimum(p, npages - 1)], 0, 0, 0)
    per_b = lambda b, p, pt: (b, 0, 0)
    in_specs = [pl.BlockSpec((1, Ls, D), per_b),
                pl.BlockSpec((1,) + cache_k.shape[1:], page),
                pl.BlockSpec((1,) + cache_v.shape[1:], page),
                pl.BlockSpec((1, P, D), per_b), pl.BlockSpec((1, P, D), per_b)]
    if mode == "diff":
        lam4, subw = extra
        in_specs += [pl.BlockSpec(lam4.shape, lambda b, p, pt: (0, 0)),
                     pl.BlockSpec(subw.shape, lambda b, p, pt: (0, 0))]
        acc_shape = (nh // 2, 2 * Ls, 2 * HEAD_DIM)
    else:
        cq, ckc, ckn = extra
        in_specs += [pl.BlockSpec((1, Ls, nh), per_b),
                     pl.BlockSpec((1, nh, P), lambda b, p, pt: (b, 0, jnp.minimum(p, npages - 1))),
                     pl.BlockSpec((1, nh, P), per_b)]
        acc_shape = (nh, Ls, HEAD_DIM)
    body = functools.partial(_decode_body, mode=mode, npages=npages, nh=nh, ls=Ls,
                             lambda_init=lambda_init)
    grid_spec = pltpu.PrefetchScalarGridSpec(
        num_scalar_prefetch=1, grid=(Bs, npages + 1), in_specs=in_specs,
        out_specs=pl.BlockSpec((1, Ls, D), per_b),
        scratch_shapes=[pltpu.VMEM((nh, Ls, 1), F32), pltpu.VMEM((nh, Ls, 1), F32),
                        pltpu.VMEM(acc_shape, F32)])
    return pl.pallas_call(
        body, grid_spec=grid_spec, out_shape=jax.ShapeDtypeStruct((Bs, Ls, D), BF16),
        compiler_params=_params("parallel", "arbitrary"), name="decode_" + mode,
    )(page_table, q_s, cache_k, cache_v, kn, vn, *extra)


def _cumsum_body(*refs, tb, per_step):
    x_refs = refs[1:1 + per_step]
    init_ref, o_ref, carry = refs[1 + per_step:]
    j = pl.program_id(1)

    @pl.when(j == 0)
    def _():
        carry[...] = init_ref[0]

    r = lax.broadcasted_iota(jnp.int32, (tb, tb), 0)
    c = lax.broadcasted_iota(jnp.int32, (tb, tb), 1)
    tri = (r >= c).astype(BF16)
    run = carry[...]
    for t, x_ref in enumerate(x_refs):
        out = _tri_dot_hi(tri, x_ref[0]) + run
        o_ref[0, t * tb:(t + 1) * tb, :] = out
        run = out[tb - 1:tb, :]
    carry[...] = run


def _cumsum_blocks(src, table, init):
    _, tb, H = src.shape
    S, nb = table.shape
    per_step = _pick(nb, 8, 1)
    xspec = lambda t_: pl.BlockSpec((1, tb, H), lambda s, j, t: (t[s, j * per_step + t_], 0, 0))
    grid_spec = pltpu.PrefetchScalarGridSpec(
        num_scalar_prefetch=1, grid=(S, nb // per_step),
        in_specs=[xspec(t_) for t_ in range(per_step)]
        + [pl.BlockSpec((1, 1, H), lambda s, j, t: (s, 0, 0))],
        out_specs=pl.BlockSpec((1, per_step * tb, H), lambda s, j, t: (s, j, 0)),
        scratch_shapes=[pltpu.VMEM((1, H), F32)])
    return pl.pallas_call(
        functools.partial(_cumsum_body, tb=tb, per_step=per_step), grid_spec=grid_spec,
        out_shape=jax.ShapeDtypeStruct((S, nb * tb, H), F32),
        compiler_params=_params("parallel", "arbitrary"), name="logf_cumsum",
    )(table, *([src] * per_step), init)


def _pool_body(x_ref, prev_ref, w_ref, scale_ref, o_ref, carry, *, tl, start, alpha):
    g = pl.program_id(1)
    t = pl.program_id(2)

    @pl.when(t == 0)
    def _():
        carry[...] = prev_ref[0]

    x = x_ref[...]
    xe = jnp.concatenate([carry[...], x], axis=0)
    s2 = xe + pltpu.roll(xe, 1, 0)
    s4 = s2 + pltpu.roll(s2, 2, 0)
    s8 = s4 + pltpu.roll(s4, 4, 0)
    s16 = s8 + pltpu.roll(s8, 8, 0)
    win = jnp.where(g == 0, s2, jnp.where(g == 1, s4, jnp.where(g == 2, s8, s16)))[POOL_HALO:]
    width = jnp.left_shift(2, g)
    pos = start + t * tl + lax.broadcasted_iota(jnp.int32, (tl, 1), 0)
    cnt = jnp.minimum(pos + 1, width).astype(F32)
    mixed = win / cnt - x
    y = _dot(mixed.astype(BF16), w_ref[0].astype(BF16)) * scale_ref[...]
    o_ref[...] = alpha * x + y
    carry[...] = xe[tl:tl + POOL_HALO]


def _pool_mixer(x, prev, w_pool, pool_scale, *, row0, B, L, start, alpha):
    D = x.shape[1]
    ng = len(POOL_WINDOWS)
    G = D // ng
    tl = _pick(L, 512, SUBLANES)
    nt = L // tl
    r0 = row0 // tl
    assert row0 % tl == 0
    body = functools.partial(_pool_body, tl=tl, start=start, alpha=alpha)
    return pl.pallas_call(
        body, grid=(B, ng, nt),
        in_specs=[pl.BlockSpec((tl, G), lambda b, g, t: (r0 + b * nt + t, g)),
                  pl.BlockSpec((1, POOL_HALO, G), lambda b, g, t: (b, 0, g)),
                  pl.BlockSpec((1, G, G), lambda b, g, t: (g, 0, 0)),
                  pl.BlockSpec((1, G), lambda b, g, t: (0, g))],
        out_specs=pl.BlockSpec((tl, G), lambda b, g, t: (b * nt + t, g)),
        out_shape=jax.ShapeDtypeStruct((B * L, D), F32),
        scratch_shapes=[pltpu.VMEM((POOL_HALO, G), F32)],
        compiler_params=_params("parallel", "parallel", "arbitrary"), name="pool_mixer",
    )(x, prev, w_pool, pool_scale.reshape(1, D))


def _conv_body(x_ref, prev_ref, w_ref, o_ref, carry, *, tl, ncb_part, qscale):
    cb = pl.program_id(1)
    t = pl.program_id(2)

    @pl.when(t == 0)
    def _():
        carry[...] = prev_ref[0]

    x = x_ref[...]
    w = w_ref[...]
    xe = jnp.concatenate([carry[...], x], axis=0)
    x1 = pltpu.roll(xe, 1, 0)[CONV_HALO:]
    x2 = pltpu.roll(xe, 2, 0)[CONV_HALO:]
    x3 = pltpu.roll(xe, 3, 0)[CONV_HALO:]
    conv = x3 * w[0:1] + x2 * w[1:2] + x1 * w[2:3] + x * w[3:4]
    act = conv * _sigmoid(conv)
    part = cb // ncb_part
    scale = jnp.where(part == 0, qscale, 1.0)
    pieces = []
    for c in range(act.shape[1] // HEAD_DIM):
        a = act[:, c * HEAD_DIM:(c + 1) * HEAD_DIM]
        n = a * lax.rsqrt(jnp.sum(a * a, axis=-1, keepdims=True) + 1e-6) * scale
        pieces.append(jnp.where(part < 2, n, a))
    o_ref[...] = jnp.concatenate(pieces, axis=1)
    carry[...] = xe[tl:tl + CONV_HALO]


def _gdn_conv(qkv, prev, conv_w, *, row0, B, L, D):
    assert conv_w.shape[0] == 4
    C3 = qkv.shape[1]
    tc = _pick(D, 512, LANES)
    tl = _pick(L, 512, SUBLANES)
    nt = L // tl
    r0 = row0 // tl
    assert row0 % tl == 0
    body = functools.partial(_conv_body, tl=tl, ncb_part=D // tc, qscale=HEAD_DIM ** -0.5)
    return pl.pallas_call(
        body, grid=(B, C3 // tc, nt),
        in_specs=[pl.BlockSpec((tl, tc), lambda b, c, t: (r0 + b * nt + t, c)),
                  pl.BlockSpec((1, CONV_HALO, tc), lambda b, c, t: (b, 0, c)),
                  pl.BlockSpec((4, tc), lambda b, c, t: (0, c))],
        out_specs=pl.BlockSpec((tl, tc), lambda b, c, t: (b * nt + t, c)),
        out_shape=jax.ShapeDtypeStruct((B * L, C3), F32),
        scratch_shapes=[pltpu.VMEM((CONV_HALO, tc), F32)],
        compiler_params=_params("parallel", "parallel", "arbitrary"), name="gdn_conv",
    )(qkv, prev, conv_w)


def _gdn_body(q_ref, k_ref, v_ref, z_ref, ab_ref, alog_ref, dtb_ref, nw_ref, s0_ref,
              o_ref, s_ref, gct_scr, *, C, HG, G4, nh, valid_len):
    hg = pl.program_id(1)
    c = pl.program_id(2)

    @pl.when(c == 0)
    def _():
        s_ref[...] = s0_ref[...]

    ab = ab_ref[...]
    row = c * C + lax.broadcasted_iota(jnp.int32, (C, 1), 0)
    valid = row < valid_len
    beta = jnp.where(valid, _sigmoid(ab[:, :nh]), 0.0)
    g = jnp.where(valid, -jnp.exp(alog_ref[...]) * _softplus(ab[:, nh:] + dtb_ref[...]), 0.0)
    ii = lax.broadcasted_iota(jnp.int32, (C, C), 0)
    jj = lax.broadcasted_iota(jnp.int32, (C, C), 1)
    gc = _tri_dot_hi((ii >= jj).astype(BF16), g)
    g_last = gc[C - 1:C, :]
    e_gc = jnp.exp(gc)
    e_kd = jnp.exp(g_last - gc)
    e_end = jnp.exp(g_last)
    gpad = jnp.concatenate([gc, jnp.zeros((C, LANES - nh), F32)], axis=1)
    gpad = jnp.concatenate([gpad, jnp.zeros((LANES - C, LANES), F32)], axis=0)
    gct_scr[...] = gpad.T
    lane_h = lax.broadcasted_iota(jnp.int32, (1, nh), 1)
    nw = nw_ref[...]
    rounds = max(int(math.ceil(math.log2(C))) - 1, 0)

    W4 = G4 * C
    ri = lax.broadcasted_iota(jnp.int32, (W4, W4), 0)
    ci = lax.broadcasted_iota(jnp.int32, (W4, W4), 1)
    same = (ri // C) == (ci // C)
    incl = same & (ri >= ci)
    strict = same & (ri > ci)
    eye = (ri == ci).astype(F32)
    rows = lambda a, t: a[t * C:(t + 1) * C]

    for grp in range(HG // G4):
        cols, grow, eend = [], [], []
        for t in range(G4):
            head = hg * HG + grp * G4 + t
            sel = lane_h == head
            col = lambda a: jnp.sum(jnp.where(sel, a, 0.0), axis=1, keepdims=True)
            cols.append((col(gc), col(beta), col(e_gc), col(e_kd)))
            eend.append(col(e_end))
            grow.append(gct_scr[pl.ds(head, 1), :][:, :C])
        stack = lambda idx: jnp.concatenate([cl[idx] for cl in cols], axis=0)
        gcol, bcol, egc, ekd = stack(0), stack(1), stack(2), stack(3)
        grow = jnp.concatenate(grow, axis=1)
        sls = [slice((grp * G4 + t) * HEAD_DIM, (grp * G4 + t + 1) * HEAD_DIM) for t in range(G4)]
        q = jnp.concatenate([q_ref[:, sl] for sl in sls], axis=0)
        k = jnp.concatenate([k_ref[:, sl] for sl in sls], axis=0)
        v = jnp.concatenate([v_ref[:, sl] for sl in sls], axis=0)
        kb = k.astype(BF16)
        decay = jnp.exp(jnp.where(incl, gcol - grow, -jnp.inf))
        kk = lax.dot_general(kb, kb, _NT, preferred_element_type=F32)
        aqk = lax.dot_general(q.astype(BF16), kb, _NT, preferred_element_type=F32) * decay
        A = jnp.where(strict, bcol * kk * decay, 0.0)
        X = eye - A
        Pw = A
        for rd in range(rounds):
            mm = _dot_hi if rd < GDN_HI_ROUNDS else _dot_lo
            Pw = mm(Pw, Pw)
            X = X + mm(X, Pw)
        rhs = jnp.concatenate([bcol * v, (bcol * egc) * k], axis=1)
        sol = _dot_hi(X, rhs)
        u = sol[:, :HEAD_DIM]
        wkqg = jnp.concatenate([sol[:, HEAD_DIM:], q * egc], axis=1).astype(BF16)
        kd = (k * ekd).astype(BF16)
        S = [s_ref[0, grp * G4 + t] for t in range(G4)]
        w_parts, o_parts = [], []
        for t in range(G4):
            Sb = S[t].astype(BF16)
            both = _dot(jnp.concatenate([rows(wkqg[:, :HEAD_DIM], t), rows(wkqg[:, HEAD_DIM:], t)],
                                        axis=0), Sb)
            w_parts.append(rows(u, t) - both[:C])
            o_parts.append(both[C:])
        w = jnp.concatenate(w_parts, axis=0)
        wb = w.astype(BF16)
        o = jnp.concatenate(o_parts, axis=0) + _dot(aqk.astype(BF16), wb)
        for t in range(G4):
            s_ref[0, grp * G4 + t] = eend[t] * S[t] + lax.dot_general(
                rows(kd, t), rows(wb, t), _TN, preferred_element_type=F32)
            ot = rows(o, t)
            zz = z_ref[:, sls[t]]
            on = ot * lax.rsqrt(jnp.mean(ot * ot, axis=-1, keepdims=True) + RMS_EPS) * nw
            o_ref[:, sls[t]] = (on * (zz * _sigmoid(zz))).astype(o_ref.dtype)


def _gdn_core(qkvn, z, ab, S0, A_log, dt_bias, norm_w, *, B, L, valid_len, zrow0):
    D = qkvn.shape[1] // 3
    nh = D // HEAD_DIM
    C = GDN_CHUNK
    HG = min(GDN_HEAD_GROUP, nh)
    G4 = min(GDN_STACK, HG)
    assert HG % G4 == 0 and nh % HG == 0
    W = HG * HEAD_DIM
    nc = L // C
    npart = D // W
    zr0 = zrow0 // C
    assert L % C == 0 and zrow0 % C == 0
    body = functools.partial(_gdn_body, C=C, HG=HG, G4=G4, nh=nh, valid_len=valid_len)
    vec = lambda n: pl.BlockSpec((1, n), lambda b, h, c: (0, 0))
    return pl.pallas_call(
        body, grid=(B, nh // HG, nc),
        in_specs=[pl.BlockSpec((C, W), lambda b, h, c: (b * nc + c, h)),
                  pl.BlockSpec((C, W), lambda b, h, c: (b * nc + c, npart + h)),
                  pl.BlockSpec((C, W), lambda b, h, c: (b * nc + c, 2 * npart + h)),
                  pl.BlockSpec((C, W), lambda b, h, c: (zr0 + b * nc + c, h)),
                  pl.BlockSpec((C, 2 * nh), lambda b, h, c: (zr0 + b * nc + c, 0)),
                  vec(nh), vec(nh), vec(HEAD_DIM),
                  pl.BlockSpec((1, HG, HEAD_DIM, HEAD_DIM), lambda b, h, c: (b, h, 0, 0))],
        out_specs=[pl.BlockSpec((C, W), lambda b, h, c: (b * nc + c, h)),
                   pl.BlockSpec((1, HG, HEAD_DIM, HEAD_DIM), lambda b, h, c: (b, h, 0, 0))],
        out_shape=[jax.ShapeDtypeStruct((B * L, D), BF16),
                   jax.ShapeDtypeStruct((B, nh, HEAD_DIM, HEAD_DIM), F32)],
        scratch_shapes=[pltpu.VMEM((LANES, LANES), F32)],
        compiler_params=_params("parallel", "parallel", "arbitrary"), name="gdn_core",
    )(qkvn, qkvn, qkvn, z, ab, A_log.reshape(1, nh), dt_bias.reshape(1, nh),
      norm_w.reshape(1, HEAD_DIM), S0)


def _rope_tables(B, L, Bs, Ls, past_len):
    pos = jnp.concatenate([jnp.tile(jnp.arange(L), B), jnp.tile(past_len + jnp.arange(Ls), Bs)])
    inv = ROPE_THETA ** (-jnp.arange(0, HEAD_DIM, 2, dtype=F32) / HEAD_DIM)
    ang = pos.astype(F32)[:, None] * inv[None, :]
    cos = jnp.cos(ang)
    sin = jnp.sin(ang)
    return jnp.concatenate([cos, cos], axis=1), jnp.concatenate([-sin, sin], axis=1)


def _kv_proj(xb, w, col_off, D, dh, epilogue, extras, name):
    tn = SUBLANES * dh
    tm_target = 1376 if tn <= 1024 else 688
    return _matmul(xb, w, col_off=col_off, n_cols=D, epilogue=epilogue, extras=extras,
                   out_dtypes=[F32, BF16], out_lane=[dh, None], name=name,
                   tm_target=tm_target, tn_target=tn, tk_target=1024)


def kernel(x_prompt, x_sample, cache_k_diff, cache_v_diff, cache_k_fox, cache_v_fox, cache_logf_fox, state_pool, state_gdn_conv, state_gdn_S, page_table, p_prompt, p_sample, ln1_g, ln1_b, ln2_g, ln2_b, w_mlp_up, w_mlp_down, w_ple_gate, w_ple_proj, w_qkv_diff, lam_q1, lam_k1, lam_q2, lam_k2, subln_diff, w_o_diff, w_in_fox, b_f_fox, w_o_fox, w_pool, pool_scale, w_in_gdn, conv_gdn, A_log_gdn, dt_bias_gdn, norm_gdn, w_o_gdn):
    B, L, D = x_prompt.shape
    Bs, Ls, _ = x_sample.shape
    depth = ln1_g.shape[0]
    Mp, Ms = B * L, Bs * Ls
    M = Mp + Ms
    nh = D // HEAD_DIM
    P = cache_k_diff.shape[1]
    npages = page_table.shape[1]
    past_len = npages * P
    alpha = (2 * depth) ** 0.25
    qk_scale = HEAD_DIM ** -0.5

    x = jnp.concatenate([x_prompt.reshape(Mp, D), x_sample.reshape(Ms, D)], axis=0)
    xb = x.astype(BF16)
    outs = {}

    def rope_extras(tm):
        cos, sin = _rope_tables(B, L, Bs, Ls, past_len)
        return [(cos, (tm, HEAD_DIM), lambda i_, j, k: (i_, 0)),
                (sin, (tm, HEAD_DIM), lambda i_, j, k: (i_, 0))]

    for i in range(depth):
        kind = i % N_MIXERS
        if kind == 0:
            lambda_init = 0.8 - 0.6 * math.exp(-0.3 * i)
            dv = 2 * HEAD_DIM
            (q,) = _matmul(xb, w_qkv_diff, col_off=0, n_cols=D, out_dtypes=[BF16],
                           epilogue=functools.partial(_epi_rope, scale=qk_scale),
                           extras=rope_extras(_mm_tiles(M, D, D)[0]), name="diff_q")
            tmk = _mm_tiles(M, D, D, tn_target=SUBLANES * HEAD_DIM, tk_target=1024)[0]
            k4, kb = _kv_proj(xb, w_qkv_diff, D, D, HEAD_DIM,
                              functools.partial(_epi_rope, scale=1.0), rope_extras(tmk), "diff_k")
            v4, vb = _kv_proj(xb, w_qkv_diff, 2 * D, D, dv, _epi_store, (), "diff_v")
            lam4 = jnp.stack([lam_q1, lam_k1, lam_q2, lam_k2]).astype(F32)
            extra = (lam4, subln_diff.reshape(1, dv))
            o_p = _flash_prompt(q, kb, vb, B=B, L=L, mode="diff", extra=extra,
                                lambda_init=lambda_init)
            o_s = _decode_attention(q[Mp:].reshape(Bs, Ls, D), kb[Mp:].reshape(Bs, Ls, D),
                                    vb[Mp:].reshape(Bs, Ls, D), cache_k_diff, cache_v_diff,
                                    page_table, mode="diff", extra=extra, lambda_init=lambda_init)
            o = jnp.concatenate([o_p, o_s.reshape(Ms, D)], axis=0)
            y = _mm_residual(o, w_o_diff, x, alpha, "diff_out")
            outs["k_diff_p"] = k4[:Mp].reshape(B, L, nh, HEAD_DIM)
            outs["v_diff_p"] = v4[:Mp].reshape(B, L, D // dv, dv)
            outs["k_diff_s"] = k4[Mp:].reshape(Bs, Ls, nh, HEAD_DIM)
            outs["v_diff_s"] = v4[Mp:].reshape(Bs, Ls, D // dv, dv)
        elif kind == 1:
            (q,) = _matmul(xb, w_in_fox, col_off=0, n_cols=D, out_dtypes=[BF16],
                           epilogue=functools.partial(_epi_scale, scale=qk_scale), name="fox_q")
            k4, kb = _kv_proj(xb, w_in_fox, D, D, HEAD_DIM, _epi_store, (), "fox_k")
            v4, vb = _kv_proj(xb, w_in_fox, 2 * D, D, HEAD_DIM, _epi_store, (), "fox_v")
            (logf,) = _matmul(xb, w_in_fox[:, 3 * D:], epilogue=_epi_logsigmoid, out_dtypes=[F32],
                              extras=[(b_f_fox.reshape(1, nh), (1, nh), lambda i_, j, k: (0, 0))],
                              name="fox_logf")
            tb = P
            tbl_p = jnp.arange(Mp // tb, dtype=jnp.int32).reshape(B, L // tb)
            c_p = _cumsum_blocks(logf[:Mp].reshape(Mp // tb, tb, nh), tbl_p,
                                 jnp.zeros((B, 1, nh), F32)).reshape(Mp, nh)
            c_c = _cumsum_blocks(cache_logf_fox, page_table, jnp.zeros((Bs, 1, nh), F32))
            logf_s = logf[Mp:].reshape(Bs, Ls, nh)
            c_n = _cumsum_blocks(jnp.pad(logf_s, ((0, 0), (0, tb - Ls), (0, 0))),
                                 jnp.arange(Bs, dtype=jnp.int32).reshape(Bs, 1),
                                 c_c[:, past_len - 1:past_len, :])
            o_p = _flash_prompt(q, kb, vb, B=B, L=L, mode="fox", extra=(c_p, c_p.T))
            extra = (c_n[:, :Ls, :], c_c.transpose(0, 2, 1), c_n.transpose(0, 2, 1))
            o_s = _decode_attention(q[Mp:].reshape(Bs, Ls, D), kb[Mp:].reshape(Bs, Ls, D),
                                    vb[Mp:].reshape(Bs, Ls, D), cache_k_fox, cache_v_fox,
                                    page_table, mode="fox", extra=extra)
            o = jnp.concatenate([o_p, o_s.reshape(Ms, D)], axis=0)
            y = _mm_residual(o, w_o_fox, x, alpha, "fox_out")
            outs["k_fox_p"] = k4[:Mp].reshape(B, L, nh, HEAD_DIM)
            outs["v_fox_p"] = v4[:Mp].reshape(B, L, nh, HEAD_DIM)
            outs["logf_fox_p"] = logf[:Mp].reshape(B, L, nh)
            outs["k_fox_s"] = k4[Mp:].reshape(Bs, Ls, nh, HEAD_DIM)
            outs["v_fox_s"] = v4[Mp:].reshape(Bs, Ls, nh, HEAD_DIM)
            outs["logf_fox_s"] = logf_s
        elif kind == 2:
            keep = state_pool.shape[1]
            prev_p = jnp.zeros((B, POOL_HALO, D), F32)
            prev_s = jnp.pad(state_pool, ((0, 0), (POOL_HALO - keep, 0), (0, 0)))
            y_p = _pool_mixer(x, prev_p, w_pool, pool_scale, row0=0, B=B, L=L, start=0, alpha=alpha)
            y_s = _pool_mixer(x, prev_s, w_pool, pool_scale, row0=Mp, B=Bs, L=Ls, start=past_len,
                              alpha=alpha)
            y = jnp.concatenate([y_p, y_s], axis=0)
            xs3 = x[Mp:].reshape(Bs, Ls, D)
            outs["pool_p"] = jnp.stack([x[(b + 1) * L - keep:(b + 1) * L] for b in range(B)])
            outs["pool_s"] = jnp.concatenate([state_pool, xs3], axis=1)[:, -keep:]
        else:
            keep = state_gdn_conv.shape[1]
            (qkv,) = _mm_plain(xb, w_in_gdn, [F32], col_off=0, n_cols=3 * D, name="gdn_qkv")
            (z,) = _mm_plain(xb, w_in_gdn, [F32], col_off=3 * D, n_cols=D, name="gdn_z")
            (ab,) = _mm_plain(xb, w_in_gdn[:, 4 * D:], [F32], name="gdn_ab")
            prev_p = jnp.zeros((B, CONV_HALO, 3 * D), F32)
            prev_s = jnp.pad(state_gdn_conv, ((0, 0), (CONV_HALO - keep, 0), (0, 0)))
            cn_p = _gdn_conv(qkv, prev_p, conv_gdn, row0=0, B=B, L=L, D=D)
            cn_s = _gdn_conv(qkv, prev_s, conv_gdn, row0=Mp, B=Bs, L=Ls, D=D)
            C = GDN_CHUNK
            o_p, S_p = _gdn_core(cn_p, z, ab, jnp.zeros((B, nh, HEAD_DIM, HEAD_DIM), F32),
                                 A_log_gdn, dt_bias_gdn, norm_gdn, B=B, L=L, valid_len=L, zrow0=0)
            padrows = lambda a: jnp.pad(a.reshape(Bs, Ls, -1), ((0, 0), (0, C - Ls), (0, 0))
                                        ).reshape(Bs * C, -1)
            o_s, S_s = _gdn_core(padrows(cn_s), padrows(z[Mp:]), padrows(ab[Mp:]), state_gdn_S,
                                 A_log_gdn, dt_bias_gdn, norm_gdn, B=Bs, L=C, valid_len=Ls, zrow0=0)
            o = jnp.concatenate([o_p, o_s.reshape(Bs, C, D)[:, :Ls].reshape(Ms, D)], axis=0)
            y = _mm_residual(o, w_o_gdn, x, alpha, "gdn_out")
            qkv_s = qkv[Mp:].reshape(Bs, Ls, 3 * D)
            outs["conv_p"] = jnp.stack([qkv[(b + 1) * L - keep:(b + 1) * L] for b in range(B)])
            outs["conv_s"] = jnp.concatenate([state_gdn_conv, qkv_s], axis=1)[:, -keep:]
            outs["S_p"] = S_p
            outs["S_s"] = S_s

        x, xb = _layer_norm(y, ln1_g[i], ln1_b[i])
        (h,) = _matmul(xb, w_mlp_up, layer=i, epilogue=_epi_relu2, out_dtypes=[BF16], name="mlp_up")
        y = _mm_residual(h, w_mlp_down, x, alpha, "mlp_down", layer=i, tn_target=1024, tk_target=1024)
        x, xb = _layer_norm(y, ln2_g[i], ln2_b[i])
        pcat = jnp.concatenate([p_prompt[i].reshape(Mp, -1), p_sample[i].reshape(Ms, -1)], axis=0)
        pdim = pcat.shape[1]
        tm, tn, _ = _mm_tiles(M, D, D, tn_target=256)
        x, xb = _matmul(
            xb, w_ple_gate, layer=i, epilogue=_epi_ple, out_dtypes=[F32, BF16], tn_target=256,
            name="ple",
            extras=[_tile_extra(x, tm, tn),
                    (pcat, (tm, pdim), lambda i_, j, k: (i_, 0)),
                    (w_ple_proj, (None, pdim, tn), lambda i_, j, k, layer=i: (layer, 0, j))])

    return (x[:Mp].reshape(B, L, D), x[Mp:].reshape(Bs, Ls, D),
            outs["k_diff_p"], outs["v_diff_p"], outs["k_diff_s"], outs["v_diff_s"],
            outs["k_fox_p"], outs["v_fox_p"], outs["logf_fox_p"],
            outs["k_fox_s"], outs["v_fox_s"], outs["logf_fox_s"],
            outs["pool_p"], outs["pool_s"], outs["conv_p"], outs["S_p"],
            outs["conv_s"], outs["S_s"])
```

```python
import functools
import math

import jax
import jax.numpy as jnp
from jax import lax
from jax.experimental import pallas as pl
from jax.experimental.pallas import tpu as pltpu

F32 = jnp.float32
BF16 = jnp.bfloat16

HEAD_DIM = 128
POOL_WINDOWS = (2, 4, 8, 16)
POOL_HALO = 16
CONV_HALO = 8
FLASH_COMPONENTS = 2
FLASH_BLOCK = 1024
GDN_CHUNK = 64
GDN_HEAD_GROUP = 32
GDN_STACK = 4
GDN_HI_ROUNDS = 2
ROPE_THETA = 10000.0
LN_EPS = 1e-5
RMS_EPS = 1e-5
N_MIXERS = 4
LANES = 128
SUBLANES = 8
VMEM_LIMIT = 56 * 1024 * 1024

_NT = (((1,), (1,)), ((), ()))
_TN = (((0,), (0,)), ((), ()))


def _params(*sem):
    return pltpu.CompilerParams(dimension_semantics=sem, vmem_limit_bytes=VMEM_LIMIT)


def _pick(dim, target, align):
    best = None
    for t in range(align, min(dim, target) + 1, align):
        if dim % t == 0:
            best = t
    return best if best is not None else dim


def _sigmoid(x):
    return 1.0 / (1.0 + jnp.exp(-x))


def _softplus(x):
    return jnp.maximum(x, 0.0) + jnp.log1p(jnp.exp(-jnp.abs(x)))


def _split3(a):
    hi = a.astype(BF16)
    r = a - hi.astype(F32)
    mid = r.astype(BF16)
    lo = (r - mid.astype(F32)).astype(BF16)
    return hi, mid, lo


def _split2(a):
    hi = a.astype(BF16)
    return hi, (a - hi.astype(F32)).astype(BF16)


def _dot(a, b):
    return jnp.dot(a, b, preferred_element_type=F32)


def _dot_hi(a, b):
    a1, a2 = _split2(a)
    b1, b2 = _split2(b)
    return _dot(a1, b1) + (_dot(a1, b2) + _dot(a2, b1))


def _dot_lo(a, b):
    return _dot(a.astype(BF16), b.astype(BF16))


def _tri_dot_hi(tri_bf16, x):
    x1, x2, x3 = _split3(x)
    return _dot(tri_bf16, x1) + _dot(tri_bf16, x2) + _dot(tri_bf16, x3)


def _mm_body(*refs, nk, n_extra, n_out, epilogue):
    x_ref, w_ref = refs[0], refs[1]
    extra = refs[2:2 + n_extra]
    outs = refs[2 + n_extra:2 + n_extra + n_out]
    if nk == 1:
        epilogue(_dot(x_ref[...].astype(BF16), w_ref[...].astype(BF16)), extra, outs)
        return
    acc_ref = refs[-1]
    k = pl.program_id(2)

    @pl.when(k == 0)
    def _():
        acc_ref[...] = jnp.zeros_like(acc_ref)

    acc_ref[...] += _dot(x_ref[...].astype(BF16), w_ref[...].astype(BF16))

    @pl.when(k == nk - 1)
    def _():
        epilogue(acc_ref[...], extra, outs)


def _mm_tiles(M, K, n_cols, tm_target=1376, tn_target=512, tk_target=None):
    if tk_target is None:
        tk_target = K if K <= 4096 else 2048
    return _pick(M, tm_target, 16), _pick(n_cols, tn_target, LANES), _pick(K, tk_target, LANES)


def _matmul(x, w, *, epilogue, out_dtypes, col_off=0, n_cols=None, extras=(), out_lane=None,
            layer=None, name="mm", **tile_kw):
    M, K = x.shape
    n_cols = w.shape[-1] - col_off if n_cols is None else n_cols
    tm, tn, tk = _mm_tiles(M, K, n_cols, **tile_kw)
    assert col_off % tn == 0
    joff = col_off // tn
    nk = K // tk
    grid = (M // tm, n_cols // tn, nk)
    xspec = (pl.BlockSpec((tm, tk), lambda i, j, k: (i, k), pipeline_mode=pl.Buffered(1))
             if nk == 1 else pl.BlockSpec((tm, tk), lambda i, j, k: (i, k)))
    if layer is None:
        wspec = pl.BlockSpec((tk, tn), lambda i, j, k: (k, j + joff))
    else:
        wspec = pl.BlockSpec((None, tk, tn), lambda i, j, k: (layer, k, j + joff))
    in_specs = [xspec, wspec]
    args = [x, w]
    for arr, bshape, imap in extras:
        in_specs.append(pl.BlockSpec(bshape, imap))
        args.append(arr)
    out_lane = [None] * len(out_dtypes) if out_lane is None else out_lane
    out_shape, out_specs = [], []
    for dt, dh in zip(out_dtypes, out_lane):
        if dh is None:
            out_shape.append(jax.ShapeDtypeStruct((M, n_cols), dt))
            out_specs.append(pl.BlockSpec((tm, tn), lambda i, j, k: (i, j)))
        else:
            out_shape.append(jax.ShapeDtypeStruct((M, n_cols // dh, dh), dt))
            out_specs.append(pl.BlockSpec((tm, tn // dh, dh), lambda i, j, k: (i, j, 0)))
    body = functools.partial(_mm_body, nk=nk, n_extra=len(extras), n_out=len(out_dtypes),
                             epilogue=epilogue)
    return pl.pallas_call(
        body, grid=grid, in_specs=in_specs, out_specs=out_specs, out_shape=out_shape,
        scratch_shapes=[pltpu.VMEM((tm, tn), F32)] if nk > 1 else [],
        compiler_params=_params("parallel", "parallel", "arbitrary"), name=name,
    )(*args)


def _put(o_ref, val):
    o_ref[...] = val.astype(o_ref.dtype).reshape(o_ref.shape)


def _epi_store(acc, extra, outs):
    for o in outs:
        _put(o, acc)


def _epi_rope(acc, extra, outs, *, scale):
    cos = extra[0][...]
    sin = extra[1][...]
    parts = []
    for c in range(acc.shape[1] // HEAD_DIM):
        xc = acc[:, c * HEAD_DIM:(c + 1) * HEAD_DIM]
        parts.append(xc * cos + pltpu.roll(xc, HEAD_DIM // 2, 1) * sin)
    r = jnp.concatenate(parts, axis=1)
    if scale != 1.0:
        r = r * scale
    for o in outs:
        _put(o, r)


def _epi_scale(acc, extra, outs, *, scale):
    for o in outs:
        _put(o, acc * scale)


def _epi_relu2(acc, extra, outs):
    h = jnp.maximum(acc, 0.0)
    _put(outs[0], h * h)


def _epi_residual(acc, extra, outs, *, alpha):
    _put(outs[0], alpha * extra[0][...] + acc)


def _epi_ple(acc, extra, outs):
    x_ref, p_ref, wp_ref = extra
    pp = _dot(p_ref[...].astype(BF16), wp_ref[...].astype(BF16))
    y = x_ref[...] + _sigmoid(acc) * pp
    for o in outs:
        _put(o, y)


def _epi_logsigmoid(acc, extra, outs):
    z = acc + extra[0][...]
    _put(outs[0], -(jnp.maximum(-z, 0.0) + jnp.log1p(jnp.exp(-jnp.abs(z)))))


def _tile_extra(arr, tm, tn):
    return (arr, (tm, tn), lambda i, j, k: (i, j))


def _mm_plain(x, w, out_dtypes, **kw):
    return _matmul(x, w, epilogue=_epi_store, out_dtypes=out_dtypes, **kw)


def _mm_residual(x, w, resid, alpha, name, layer=None, **tile_kw):
    M, N = resid.shape
    tm, tn, _ = _mm_tiles(M, x.shape[1], N, **tile_kw)
    return _matmul(x, w, epilogue=functools.partial(_epi_residual, alpha=alpha), out_dtypes=[F32],
                   extras=[_tile_extra(resid, tm, tn)], layer=layer, name=name, **tile_kw)[0]


def _ln_body(y_ref, g_ref, b_ref, o32_ref, o16_ref):
    y = y_ref[...]
    mu = jnp.mean(y, axis=-1, keepdims=True)
    d = y - mu
    var = jnp.mean(d * d, axis=-1, keepdims=True)
    out = d * lax.rsqrt(var + LN_EPS) * g_ref[...] + b_ref[...]
    o32_ref[...] = out
    o16_ref[...] = out.astype(BF16)


def _layer_norm(y, g, b):
    M, D = y.shape
    tm = _pick(M, 256, 16)
    row = pl.BlockSpec((tm, D), lambda i: (i, 0))
    vec = pl.BlockSpec((1, D), lambda i: (0, 0))
    return pl.pallas_call(
        _ln_body, grid=(M // tm,), in_specs=[row, vec, vec], out_specs=[row, row],
        out_shape=[jax.ShapeDtypeStruct((M, D), F32), jax.ShapeDtypeStruct((M, D), BF16)],
        compiler_params=_params("parallel"), name="layer_norm",
    )(y, g.reshape(1, D), b.reshape(1, D))


def _lambda_full(lam_ref, lambda_init):
    lam = lam_ref[...]
    e1 = jnp.exp(jnp.sum(lam[0:1] * lam[1:2], axis=-1, keepdims=True))
    e2 = jnp.exp(jnp.sum(lam[2:3] * lam[3:4], axis=-1, keepdims=True))
    return e1 - e2 + lambda_init


def _diff_finish(o0, o1, lam, subw, lambda_init):
    o = o0 - lam * o1
    o = o * lax.rsqrt(jnp.mean(o * o, axis=-1, keepdims=True) + RMS_EPS) * subw
    return o * (1.0 - lambda_init)


def _flash_body(*refs, mode, nk, tq, tk, nc, lambda_init):
    if mode == "diff":
        q_ref, k_ref, v_ref, lam_ref, subw_ref, o_ref, m_scr, l_scr, acc_scr = refs
    else:
        q_ref, k_ref, v_ref, cq_ref, ck_ref, o_ref, m_scr, l_scr, acc_scr = refs
    qi = pl.program_id(2)
    ki = pl.program_id(3)
    dv = acc_scr.shape[1]

    @pl.when(ki == 0)
    def _():
        m_scr[...] = jnp.full(m_scr.shape, -jnp.inf, F32)
        l_scr[...] = jnp.zeros(l_scr.shape, F32)
        acc_scr[...] = jnp.zeros(acc_scr.shape, F32)

    def update(on_diagonal):
        q = q_ref[...]
        k = k_ref[...]
        v = v_ref[...]
        if on_diagonal:
            mask = (lax.broadcasted_iota(jnp.int32, (tk, tq), 0)
                    <= lax.broadcasted_iota(jnp.int32, (tk, tq), 1))
        for c in range(nc):
            sl = slice(c * HEAD_DIM, (c + 1) * HEAD_DIM)
            s = lax.dot_general(k[:, sl], q[:, sl], _NT, preferred_element_type=F32)
            if mode == "fox":
                s = s + (cq_ref[0][c:c + 1, :] - ck_ref[0][:, c:c + 1])
            if on_diagonal:
                s = jnp.where(mask, s, -jnp.inf)
            m_old = m_scr[c]
            m_new = jnp.maximum(m_old, jnp.max(s, axis=0, keepdims=True))
            alpha = jnp.exp(m_old - m_new)
            p = jnp.exp(s - m_new)
            l_scr[c] = alpha * l_scr[c] + jnp.sum(p, axis=0, keepdims=True)
            vv = v[:, (c // 2) * dv:(c // 2 + 1) * dv] if mode == "diff" else v[:, sl]
            acc_scr[c] = alpha * acc_scr[c] + lax.dot_general(
                vv, p.astype(BF16), _TN, preferred_element_type=F32)
            m_scr[c] = m_new

    @pl.when(ki < qi)
    def _():
        update(False)

    @pl.when(ki == qi)
    def _():
        update(True)

    @pl.when(ki == nk - 1)
    def _():
        o = [acc_scr[c] / l_scr[c] for c in range(nc)]
        if mode == "diff":
            lam = _lambda_full(lam_ref, lambda_init)
            outs = []
            for c in range(0, nc, 2):
                d = o[c] - lam * o[c + 1]
                d = d * lax.rsqrt(jnp.mean(d * d, axis=0, keepdims=True) + RMS_EPS)
                outs.append(d.T * subw_ref[...] * (1.0 - lambda_init))
        else:
            outs = [oc.T for oc in o]
        o_ref[...] = jnp.concatenate(outs, axis=1).astype(o_ref.dtype)


def _flash_prompt(q, k, v, *, B, L, mode, extra, lambda_init=0.0):
    D = q.shape[1]
    nc = FLASH_COMPONENTS
    W = nc * HEAD_DIM
    assert D % W == 0
    tq = tk = _pick(L, FLASH_BLOCK, LANES)
    nq = L // tq
    grid = (B, D // W, nq, nq)
    qspec = pl.BlockSpec((tq, W), lambda b, h, qi, ki: (b * nq + qi, h))
    kspec = pl.BlockSpec((tk, W), lambda b, h, qi, ki: (b * nq + jnp.minimum(ki, qi), h))
    if mode == "diff":
        lam4, subw = extra
        especs = [pl.BlockSpec(lam4.shape, lambda b, h, qi, ki: (0, 0)),
                  pl.BlockSpec(subw.shape, lambda b, h, qi, ki: (0, 0))]
        dv = 2 * HEAD_DIM
    else:
        c_rows, c_cols = extra
        extra = (c_cols.reshape(D // W, nc, B * L),
                 c_rows.reshape(B * L, D // W, nc).transpose(1, 0, 2))
        especs = [pl.BlockSpec((1, nc, tq), lambda b, h, qi, ki: (h, 0, b * nq + qi)),
                  pl.BlockSpec((1, tk, nc), lambda b, h, qi, ki: (h, b * nq + jnp.minimum(ki, qi), 0))]
        dv = HEAD_DIM
    body = functools.partial(_flash_body, mode=mode, nk=nq, tq=tq, tk=tk, nc=nc,
                             lambda_init=lambda_init)
    return pl.pallas_call(
        body, grid=grid, in_specs=[qspec, kspec, kspec] + especs,
        out_specs=pl.BlockSpec((tq, W), lambda b, h, qi, ki: (b * nq + qi, h)),
        out_shape=jax.ShapeDtypeStruct((B * L, D), BF16),
        scratch_shapes=[pltpu.VMEM((nc, 1, tq), F32), pltpu.VMEM((nc, 1, tq), F32),
                        pltpu.VMEM((nc, dv, tq), F32)],
        compiler_params=_params("parallel", "parallel", "parallel", "arbitrary"),
        name="flash_" + mode,
    )(q, k, v, *extra)


def _decode_body(*refs, mode, npages, nh, ls, lambda_init):
    pt_ref = refs[0]
    if mode == "diff":
        (q_ref, kc_ref, vc_ref, kn_ref, vn_ref, lam_ref, subw_ref,
         o_ref, m_scr, l_scr, acc_scr) = refs[1:]
    else:
        (q_ref, kc_ref, vc_ref, kn_ref, vn_ref, cq_ref, ckc_ref, ckn_ref,
         o_ref, m_scr, l_scr, acc_scr) = refs[1:]
    del pt_ref
    p = pl.program_id(1)
    P = kc_ref.shape[1]
    dv = acc_scr.shape[-1]

    @pl.when(p == 0)
    def _():
        m_scr[...] = jnp.full(m_scr.shape, -jnp.inf, F32)
        l_scr[...] = jnp.zeros(l_scr.shape, F32)
        acc_scr[...] = jnp.zeros(acc_scr.shape, F32)

    def step(k_of, v_of, ck, mask):
        q = q_ref[0]
        probs = []
        for h in range(nh):
            s = lax.dot_general(q[:, h * HEAD_DIM:(h + 1) * HEAD_DIM], k_of(h), _NT,
                                preferred_element_type=F32)
            if mode == "fox":
                s = s + (cq_ref[0][:, h:h + 1] - ck[h:h + 1, :])
            if mask is not None:
                s = jnp.where(mask, s, -jnp.inf)
            m_old = m_scr[h]
            m_new = jnp.maximum(m_old, jnp.max(s, axis=-1, keepdims=True))
            alpha = jnp.exp(m_old - m_new)
            pr = jnp.exp(s - m_new)
            l_scr[h] = alpha * l_scr[h] + jnp.sum(pr, axis=-1, keepdims=True)
            m_scr[h] = m_new
            probs.append((alpha, pr.astype(BF16)))
        if mode == "diff":
            for hh in range(nh // 2):
                (a0, p0), (a1, p1) = probs[2 * hh], probs[2 * hh + 1]
                pv = _dot(jnp.concatenate([p0, p1], axis=0), v_of(hh))
                al = jnp.concatenate([a0, a1], axis=0)
                acc_scr[hh] = al * acc_scr[hh] + pv
        else:
            for h in range(nh):
                a0, p0 = probs[h]
                acc_scr[h] = a0 * acc_scr[h] + _dot(p0, v_of(h))

    @pl.when(p < npages)
    def _():
        kc = pltpu.einshape("khd->hkd", kc_ref[0].astype(BF16))
        vc = pltpu.einshape("khd->hkd", vc_ref[0].astype(BF16))
        ck = ckc_ref[0] if mode == "fox" else None
        step(lambda h: kc[h], lambda h: vc[h], ck, None)

    @pl.when(p == npages)
    def _():
        row = lax.broadcasted_iota(jnp.int32, (ls, P), 0)
        col = lax.broadcasted_iota(jnp.int32, (ls, P), 1)
        ck = ckn_ref[0] if mode == "fox" else None
        kn = kn_ref[0]
        vn = vn_ref[0]
        step(lambda h: kn[:, h * HEAD_DIM:(h + 1) * HEAD_DIM],
             lambda h: vn[:, h * dv:(h + 1) * dv], ck, col <= row)
        outs = []
        if mode == "diff":
            lam = _lambda_full(lam_ref, lambda_init)
            for hh in range(nh // 2):
                acc = acc_scr[hh]
                o0 = acc[:ls] / l_scr[2 * hh]
                o1 = acc[ls:] / l_scr[2 * hh + 1]
                outs.append(_diff_finish(o0, o1, lam, subw_ref[...], lambda_init))
        else:
            for h in range(nh):
                outs.append(acc_scr[h] / l_scr[h])
        o_ref[0] = jnp.concatenate(outs, axis=1).astype(o_ref.dtype)


def _decode_attention(q_s, k_new, v_new, cache_k, cache_v, page_table, *, mode, extra,
                      lambda_init=0.0):
    Bs, Ls, D = q_s.shape
    P = cache_k.shape[1]
    npages = page_table.shape[1]
    nh = D // HEAD_DIM
    pad = ((0, 0), (0, P - Ls), (0, 0))
    kn = jnp.pad(k_new, pad)
    vn = jnp.pad(v_new, pad)
    page = lambda b, p, pt: (pt[b, jnp.minimum(p, npages - 1)], 0, 0, 0)
    per_b = lambda b, p, pt: (b, 0, 0)
    in_specs = [pl.BlockSpec((1, Ls, D), per_b),
                pl.BlockSpec((1,) + cache_k.shape[1:], page),
                pl.BlockSpec((1,) + cache_v.shape[1:], page),
                pl.BlockSpec((1, P, D), per_b), pl.BlockSpec((1, P, D), per_b)]
    if mode == "diff":
        lam4, subw = extra
        in_specs += [pl.BlockSpec(lam4.shape, lambda b, p, pt: (0, 0)),
                     pl.BlockSpec(subw.shape, lambda b, p, pt: (0, 0))]
        acc_shape = (nh // 2, 2 * Ls, 2 * HEAD_DIM)
    else:
        cq, ckc, ckn = extra
        in_specs += [pl.BlockSpec((1, Ls, nh), per_b),
                     pl.BlockSpec((1, nh, P), lambda b, p, pt: (b, 0, jnp.minimum(p, npages - 1))),
                     pl.BlockSpec((1, nh, P), per_b)]
        acc_shape = (nh, Ls, HEAD_DIM)
    body = functools.partial(_decode_body, mode=mode, npages=npages, nh=nh, ls=Ls,
                             lambda_init=lambda_init)
    grid_spec = pltpu.PrefetchScalarGridSpec(
        num_scalar_prefetch=1, grid=(Bs, npages + 1), in_specs=in_specs,
        out_specs=pl.BlockSpec((1, Ls, D), per_b),
        scratch_shapes=[pltpu.VMEM((nh, Ls, 1), F32), pltpu.VMEM((nh, Ls, 1), F32),
                        pltpu.VMEM(acc_shape, F32)])
    return pl.pallas_call(
        body, grid_spec=grid_spec, out_shape=jax.ShapeDtypeStruct((Bs, Ls, D), BF16),
        compiler_params=_params("parallel", "arbitrary"), name="decode_" + mode,
    )(page_table, q_s, cache_k, cache_v, kn, vn, *extra)


def _cumsum_body(*refs, tb, per_step):
    x_refs = refs[1:1 + per_step]
    init_ref, o_ref, carry = refs[1 + per_step:]
    j = pl.program_id(1)

    @pl.when(j == 0)
    def _():
        carry[...] = init_ref[0]

    r = lax.broadcasted_iota(jnp.int32, (tb, tb), 0)
    c = lax.broadcasted_iota(jnp.int32, (tb, tb), 1)
    tri = (r >= c).astype(BF16)
    run = carry[...]
    for t, x_ref in enumerate(x_refs):
        out = _tri_dot_hi(tri, x_ref[0]) + run
        o_ref[0, t * tb:(t + 1) * tb, :] = out
        run = out[tb - 1:tb, :]
    carry[...] = run


def _cumsum_blocks(src, table, init):
    _, tb, H = src.shape
    S, nb = table.shape
    per_step = _pick(nb, 8, 1)
    xspec = lambda t_: pl.BlockSpec((1, tb, H), lambda s, j, t: (t[s, j * per_step + t_], 0, 0))
    grid_spec = pltpu.PrefetchScalarGridSpec(
        num_scalar_prefetch=1, grid=(S, nb // per_step),
        in_specs=[xspec(t_) for t_ in range(per_step)]
        + [pl.BlockSpec((1, 1, H), lambda s, j, t: (s, 0, 0))],
        out_specs=pl.BlockSpec((1, per_step * tb, H), lambda s, j, t: (s, j, 0)),
        scratch_shapes=[pltpu.VMEM((1, H), F32)])
    return pl.pallas_call(
        functools.partial(_cumsum_body, tb=tb, per_step=per_step), grid_spec=grid_spec,
        out_shape=jax.ShapeDtypeStruct((S, nb * tb, H), F32),
        compiler_params=_params("parallel", "arbitrary"), name="logf_cumsum",
    )(table, *([src] * per_step), init)


def _pool_body(x_ref, prev_ref, w_ref, scale_ref, o_ref, carry, *, tl, start, alpha):
    g = pl.program_id(1)
    t = pl.program_id(2)

    @pl.when(t == 0)
    def _():
        carry[...] = prev_ref[0]

    x = x_ref[...]
    xe = jnp.concatenate([carry[...], x], axis=0)
    s2 = xe + pltpu.roll(xe, 1, 0)
    s4 = s2 + pltpu.roll(s2, 2, 0)
    s8 = s4 + pltpu.roll(s4, 4, 0)
    s16 = s8 + pltpu.roll(s8, 8, 0)
    win = jnp.where(g == 0, s2, jnp.where(g == 1, s4, jnp.where(g == 2, s8, s16)))[POOL_HALO:]
    width = jnp.left_shift(2, g)
    pos = start + t * tl + lax.broadcasted_iota(jnp.int32, (tl, 1), 0)
    cnt = jnp.minimum(pos + 1, width).astype(F32)
    mixed = win / cnt - x
    y = _dot(mixed.astype(BF16), w_ref[0].astype(BF16)) * scale_ref[...]
    o_ref[...] = alpha * x + y
    carry[...] = xe[tl:tl + POOL_HALO]


def _pool_mixer(x, prev, w_pool, pool_scale, *, row0, B, L, start, alpha):
    D = x.shape[1]
    ng = len(POOL_WINDOWS)
    G = D // ng
    tl = _pick(L, 512, SUBLANES)
    nt = L // tl
    r0 = row0 // tl
    assert row0 % tl == 0
    body = functools.partial(_pool_body, tl=tl, start=start, alpha=alpha)
    return pl.pallas_call(
        body, grid=(B, ng, nt),
        in_specs=[pl.BlockSpec((tl, G), lambda b, g, t: (r0 + b * nt + t, g)),
                  pl.BlockSpec((1, POOL_HALO, G), lambda b, g, t: (b, 0, g)),
                  pl.BlockSpec((1, G, G), lambda b, g, t: (g, 0, 0)),
                  pl.BlockSpec((1, G), lambda b, g, t: (0, g))],
        out_specs=pl.BlockSpec((tl, G), lambda b, g, t: (b * nt + t, g)),
        out_shape=jax.ShapeDtypeStruct((B * L, D), F32),
        scratch_shapes=[pltpu.VMEM((POOL_HALO, G), F32)],
        compiler_params=_params("parallel", "parallel", "arbitrary"), name="pool_mixer",
    )(x, prev, w_pool, pool_scale.reshape(1, D))


def _conv_body(x_ref, prev_ref, w_ref, o_ref, carry, *, tl, ncb_part, qscale):
    cb = pl.program_id(1)
    t = pl.program_id(2)

    @pl.when(t == 0)
    def _():
        carry[...] = prev_ref[0]

    x = x_ref[...]
    w = w_ref[...]
    xe = jnp.concatenate([carry[...], x], axis=0)
    x1 = pltpu.roll(xe, 1, 0)[CONV_HALO:]
    x2 = pltpu.roll(xe, 2, 0)[CONV_HALO:]
    x3 = pltpu.roll(xe, 3, 0)[CONV_HALO:]
    conv = x3 * w[0:1] + x2 * w[1:2] + x1 * w[2:3] + x * w[3:4]
    act = conv * _sigmoid(conv)
    part = cb // ncb_part
    scale = jnp.where(part == 0, qscale, 1.0)
    pieces = []
    for c in range(act.shape[1] // HEAD_DIM):
        a = act[:, c * HEAD_DIM:(c + 1) * HEAD_DIM]
        n = a * lax.rsqrt(jnp.sum(a * a, axis=-1, keepdims=True) + 1e-6) * scale
        pieces.append(jnp.where(part < 2, n, a))
    o_ref[...] = jnp.concatenate(pieces, axis=1)
    carry[...] = xe[tl:tl + CONV_HALO]


def _gdn_conv(qkv, prev, conv_w, *, row0, B, L, D):
    assert conv_w.shape[0] == 4
    C3 = qkv.shape[1]
    tc = _pick(D, 512, LANES)
    tl = _pick(L, 512, SUBLANES)
    nt = L // tl
    r0 = row0 // tl
    assert row0 % tl == 0
    body = functools.partial(_conv_body, tl=tl, ncb_part=D // tc, qscale=HEAD_DIM ** -0.5)
    return pl.pallas_call(
        body, grid=(B, C3 // tc, nt),
        in_specs=[pl.BlockSpec((tl, tc), lambda b, c, t: (r0 + b * nt + t, c)),
                  pl.BlockSpec((1, CONV_HALO, tc), lambda b, c, t: (b, 0, c)),
                  pl.BlockSpec((4, tc), lambda b, c, t: (0, c))],
        out_specs=pl.BlockSpec((tl, tc), lambda b, c, t: (b * nt + t, c)),
        out_shape=jax.ShapeDtypeStruct((B * L, C3), F32),
        scratch_shapes=[pltpu.VMEM((CONV_HALO, tc), F32)],
        compiler_params=_params("parallel", "parallel", "arbitrary"), name="gdn_conv",
    )(qkv, prev, conv_w)


def _gdn_body(q_ref, k_ref, v_ref, z_ref, ab_ref, alog_ref, dtb_ref, nw_ref, s0_ref,
              o_ref, s_ref, gct_scr, *, C, HG, G4, nh, valid_len):
    hg = pl.program_id(1)
    c = pl.program_id(2)

    @pl.when(c == 0)
    def _():
        s_ref[...] = s0_ref[...]

    ab = ab_ref[...]
    row = c * C + lax.broadcasted_iota(jnp.int32, (C, 1), 0)
    valid = row < valid_len
    beta = jnp.where(valid, _sigmoid(ab[:, :nh]), 0.0)
    g = jnp.where(valid, -jnp.exp(alog_ref[...]) * _softplus(ab[:, nh:] + dtb_ref[...]), 0.0)
    ii = lax.broadcasted_iota(jnp.int32, (C, C), 0)
    jj = lax.broadcasted_iota(jnp.int32, (C, C), 1)
    gc = _tri_dot_hi((ii >= jj).astype(BF16), g)
    g_last = gc[C - 1:C, :]
    e_gc = jnp.exp(gc)
    e_kd = jnp.exp(g_last - gc)
    e_end = jnp.exp(g_last)
    gpad = jnp.concatenate([gc, jnp.zeros((C, LANES - nh), F32)], axis=1)
    gpad = jnp.concatenate([gpad, jnp.zeros((LANES - C, LANES), F32)], axis=0)
    gct_scr[...] = gpad.T
    lane_h = lax.broadcasted_iota(jnp.int32, (1, nh), 1)
    nw = nw_ref[...]
    rounds = max(int(math.ceil(math.log2(C))) - 1, 0)

    W4 = G4 * C
    ri = lax.broadcasted_iota(jnp.int32, (W4, W4), 0)
    ci = lax.broadcasted_iota(jnp.int32, (W4, W4), 1)
    same = (ri // C) == (ci // C)
    incl = same & (ri >= ci)
    strict = same & (ri > ci)
    eye = (ri == ci).astype(F32)
    rows = lambda a, t: a[t * C:(t + 1) * C]

    for grp in range(HG // G4):
        cols, grow, eend = [], [], []
        for t in range(G4):
            head = hg * HG + grp * G4 + t
            sel = lane_h == head
            col = lambda a: jnp.sum(jnp.where(sel, a, 0.0), axis=1, keepdims=True)
            cols.append((col(gc), col(beta), col(e_gc), col(e_kd)))
            eend.append(col(e_end))
            grow.append(gct_scr[pl.ds(head, 1), :][:, :C])
        stack = lambda idx: jnp.concatenate([cl[idx] for cl in cols], axis=0)
        gcol, bcol, egc, ekd = stack(0), stack(1), stack(2), stack(3)
        grow = jnp.concatenate(grow, axis=1)
        sls = [slice((grp * G4 + t) * HEAD_DIM, (grp * G4 + t + 1) * HEAD_DIM) for t in range(G4)]
        q = jnp.concatenate([q_ref[:, sl] for sl in sls], axis=0)
        k = jnp.concatenate([k_ref[:, sl] for sl in sls], axis=0)
        v = jnp.concatenate([v_ref[:, sl] for sl in sls], axis=0)
        kb = k.astype(BF16)
        decay = jnp.exp(jnp.where(incl, gcol - grow, -jnp.inf))
        kk = lax.dot_general(kb, kb, _NT, preferred_element_type=F32)
        aqk = lax.dot_general(q.astype(BF16), kb, _NT, preferred_element_type=F32) * decay
        A = jnp.where(strict, bcol * kk * decay, 0.0)
        X = eye - A
        Pw = A
        for rd in range(rounds):
            mm = _dot_hi if rd < GDN_HI_ROUNDS else _dot_lo
            Pw = mm(Pw, Pw)
            X = X + mm(X, Pw)
        rhs = jnp.concatenate([bcol * v, (bcol * egc) * k], axis=1)
        sol = _dot_hi(X, rhs)
        u = sol[:, :HEAD_DIM]
        wkqg = jnp.concatenate([sol[:, HEAD_DIM:], q * egc], axis=1).astype(BF16)
        kd = (k * ekd).astype(BF16)
        S = [s_ref[0, grp * G4 + t] for t in range(G4)]
        w_parts, o_parts = [], []
        for t in range(G4):
            Sb = S[t].astype(BF16)
            both = _dot(jnp.concatenate([rows(wkqg[:, :HEAD_DIM], t), rows(wkqg[:, HEAD_DIM:], t)],
                                        axis=0), Sb)
            w_parts.append(rows(u, t) - both[:C])
            o_parts.append(both[C:])
        w = jnp.concatenate(w_parts, axis=0)
        wb = w.astype(BF16)
        o = jnp.concatenate(o_parts, axis=0) + _dot(aqk.astype(BF16), wb)
        for t in range(G4):
            s_ref[0, grp * G4 + t] = eend[t] * S[t] + lax.dot_general(
                rows(kd, t), rows(wb, t), _TN, preferred_element_type=F32)
            ot = rows(o, t)
            zz = z_ref[:, sls[t]]
            on = ot * lax.rsqrt(jnp.mean(ot * ot, axis=-1, keepdims=True) + RMS_EPS) * nw
            o_ref[:, sls[t]] = (on * (zz * _sigmoid(zz))).astype(o_ref.dtype)


def _gdn_core(qkvn, z, ab, S0, A_log, dt_bias, norm_w, *, B, L, valid_len, zrow0):
    D = qkvn.shape[1] // 3
    nh = D // HEAD_DIM
    C = GDN_CHUNK
    HG = min(GDN_HEAD_GROUP, nh)
    G4 = min(GDN_STACK, HG)
    assert HG % G4 == 0 and nh % HG == 0
    W = HG * HEAD_DIM
    nc = L // C
    npart = D // W
    zr0 = zrow0 // C
    assert L % C == 0 and zrow0 % C == 0
    body = functools.partial(_gdn_body, C=C, HG=HG, G4=G4, nh=nh, valid_len=valid_len)
    vec = lambda n: pl.BlockSpec((1, n), lambda b, h, c: (0, 0))
    return pl.pallas_call(
        body, grid=(B, nh // HG, nc),
        in_specs=[pl.BlockSpec((C, W), lambda b, h, c: (b * nc + c, h)),
                  pl.BlockSpec((C, W), lambda b, h, c: (b * nc + c, npart + h)),
                  pl.BlockSpec((C, W), lambda b, h, c: (b * nc + c, 2 * npart + h)),
                  pl.BlockSpec((C, W), lambda b, h, c: (zr0 + b * nc + c, h)),
                  pl.BlockSpec((C, 2 * nh), lambda b, h, c: (zr0 + b * nc + c, 0)),
                  vec(nh), vec(nh), vec(HEAD_DIM),
                  pl.BlockSpec((1, HG, HEAD_DIM, HEAD_DIM), lambda b, h, c: (b, h, 0, 0))],
        out_specs=[pl.BlockSpec((C, W), lambda b, h, c: (b * nc + c, h)),
                   pl.BlockSpec((1, HG, HEAD_DIM, HEAD_DIM), lambda b, h, c: (b, h, 0, 0))],
        out_shape=[jax.ShapeDtypeStruct((B * L, D), BF16),
                   jax.ShapeDtypeStruct((B, nh, HEAD_DIM, HEAD_DIM), F32)],
        scratch_shapes=[pltpu.VMEM((LANES, LANES), F32)],
        compiler_params=_params("parallel", "parallel", "arbitrary"), name="gdn_core",
    )(qkvn, qkvn, qkvn, z, ab, A_log.reshape(1, nh), dt_bias.reshape(1, nh),
      norm_w.reshape(1, HEAD_DIM), S0)


def _rope_tables(B, L, Bs, Ls, past_len):
    pos = jnp.concatenate([jnp.tile(jnp.arange(L), B), jnp.tile(past_len + jnp.arange(Ls), Bs)])
    inv = ROPE_THETA ** (-jnp.arange(0, HEAD_DIM, 2, dtype=F32) / HEAD_DIM)
    ang = pos.astype(F32)[:, None] * inv[None, :]
    cos = jnp.cos(ang)
    sin = jnp.sin(ang)
    return jnp.concatenate([cos, cos], axis=1), jnp.concatenate([-sin, sin], axis=1)


def _kv_proj(xb, w, col_off, D, dh, epilogue, extras, name):
    tn = SUBLANES * dh
    tm_target = 1376 if tn <= 1024 else 688
    return _matmul(xb, w, col_off=col_off, n_cols=D, epilogue=epilogue, extras=extras,
                   out_dtypes=[F32, BF16], out_lane=[dh, None], name=name,
                   tm_target=tm_target, tn_target=tn, tk_target=1024)


def kernel(x_prompt, x_sample, cache_k_diff, cache_v_diff, cache_k_fox, cache_v_fox, cache_logf_fox, state_pool, state_gdn_conv, state_gdn_S, page_table, p_prompt, p_sample, ln1_g, ln1_b, ln2_g, ln2_b, w_mlp_up, w_mlp_down, w_ple_gate, w_ple_proj, w_qkv_diff, lam_q1, lam_k1, lam_q2, lam_k2, subln_diff, w_o_diff, w_in_fox, b_f_fox, w_o_fox, w_pool, pool_scale, w_in_gdn, conv_gdn, A_log_gdn, dt_bias_gdn, norm_gdn, w_o_gdn):
    B, L, D = x_prompt.shape
    Bs, Ls, _ = x_sample.shape
    depth = ln1_g.shape[0]
    Mp, Ms = B * L, Bs * Ls
    M = Mp + Ms
    nh = D // HEAD_DIM
    P = cache_k_diff.shape[1]
    npages = page_table.shape[1]
    past_len = npages * P
    alpha = (2 * depth) ** 0.25
    qk_scale = HEAD_DIM ** -0.5

    x = jnp.concatenate([x_prompt.reshape(Mp, D), x_sample.reshape(Ms, D)], axis=0)
    xb = x.astype(BF16)
    outs = {}

    def rope_extras(tm):
        cos, sin = _rope_tables(B, L, Bs, Ls, past_len)
        return [(cos, (tm, HEAD_DIM), lambda i_, j, k: (i_, 0)),
                (sin, (tm, HEAD_DIM), lambda i_, j, k: (i_, 0))]

    for i in range(depth):
        kind = i % N_MIXERS
        if kind == 0:
            lambda_init = 0.8 - 0.6 * math.exp(-0.3 * i)
            dv = 2 * HEAD_DIM
            (q,) = _matmul(xb, w_qkv_diff, col_off=0, n_cols=D, out_dtypes=[BF16],
                           epilogue=functools.partial(_epi_rope, scale=qk_scale),
                           extras=rope_extras(_mm_tiles(M, D, D)[0]), name="diff_q")
            tmk = _mm_tiles(M, D, D, tn_target=SUBLANES * HEAD_DIM, tk_target=1024)[0]
            k4, kb = _kv_proj(xb, w_qkv_diff, D, D, HEAD_DIM,
                              functools.partial(_epi_rope, scale=1.0), rope_extras(tmk), "diff_k")
            v4, vb = _kv_proj(xb, w_qkv_diff, 2 * D, D, dv, _epi_store, (), "diff_v")
            lam4 = jnp.stack([lam_q1, lam_k1, lam_q2, lam_k2]).astype(F32)
            extra = (lam4, subln_diff.reshape(1, dv))
            o_p = _flash_prompt(q, kb, vb, B=B, L=L, mode="diff", extra=extra,
                                lambda_init=lambda_init)
            o_s = _decode_attention(q[Mp:].reshape(Bs, Ls, D), kb[Mp:].reshape(Bs, Ls, D),
                                    vb[Mp:].reshape(Bs, Ls, D), cache_k_diff, cache_v_diff,
                                    page_table, mode="diff", extra=extra, lambda_init=lambda_init)
            o = jnp.concatenate([o_p, o_s.reshape(Ms, D)], axis=0)
            y = _mm_residual(o, w_o_diff, x, alpha, "diff_out")
            outs["k_diff_p"] = k4[:Mp].reshape(B, L, nh, HEAD_DIM)
            outs["v_diff_p"] = v4[:Mp].reshape(B, L, D // dv, dv)
            outs["k_diff_s"] = k4[Mp:].reshape(Bs, Ls, nh, HEAD_DIM)
            outs["v_diff_s"] = v4[Mp:].reshape(Bs, Ls, D // dv, dv)
        elif kind == 1:
            (q,) = _matmul(xb, w_in_fox, col_off=0, n_cols=D, out_dtypes=[BF16],
                           epilogue=functools.partial(_epi_scale, scale=qk_scale), name="fox_q")
            k4, kb = _kv_proj(xb, w_in_fox, D, D, HEAD_DIM, _epi_store, (), "fox_k")
            v4, vb = _kv_proj(xb, w_in_fox, 2 * D, D, HEAD_DIM, _epi_store, (), "fox_v")
            (logf,) = _matmul(xb, w_in_fox[:, 3 * D:], epilogue=_epi_logsigmoid, out_dtypes=[F32],
                              extras=[(b_f_fox.reshape(1, nh), (1, nh), lambda i_, j, k: (0, 0))],
                              name="fox_logf")
            tb = P
            tbl_p = jnp.arange(Mp // tb, dtype=jnp.int32).reshape(B, L // tb)
            c_p = _cumsum_blocks(logf[:Mp].reshape(Mp // tb, tb, nh), tbl_p,
                                 jnp.zeros((B, 1, nh), F32)).reshape(Mp, nh)
            c_c = _cumsum_blocks(cache_logf_fox, page_table, jnp.zeros((Bs, 1, nh), F32))
            logf_s = logf[Mp:].reshape(Bs, Ls, nh)
            c_n = _cumsum_blocks(jnp.pad(logf_s, ((0, 0), (0, tb - Ls), (0, 0))),
                                 jnp.arange(Bs, dtype=jnp.int32).reshape(Bs, 1),
                                 c_c[:, past_len - 1:past_len, :])
            o_p = _flash_prompt(q, kb, vb, B=B, L=L, mode="fox", extra=(c_p, c_p.T))
            extra = (c_n[:, :Ls, :], c_c.transpose(0, 2, 1), c_n.transpose(0, 2, 1))
            o_s = _decode_attention(q[Mp:].reshape(Bs, Ls, D), kb[Mp:].reshape(Bs, Ls, D),
                                    vb[Mp:].reshape(Bs, Ls, D), cache_k_fox, cache_v_fox,
                                    page_table, mode="fox", extra=extra)
            o = jnp.concatenate([o_p, o_s.reshape(Ms, D)], axis=0)
            y = _mm_residual(o, w_o_fox, x, alpha, "fox_out")
            outs["k_fox_p"] = k4[:Mp].reshape(B, L, nh, HEAD_DIM)
            outs["v_fox_p"] = v4[:Mp].reshape(B, L, nh, HEAD_DIM)
            outs["logf_fox_p"] = logf[:Mp].reshape(B, L, nh)
            outs["k_fox_s"] = k4[Mp:].reshape(Bs, Ls, nh, HEAD_DIM)
            outs["v_fox_s"] = v4[Mp:].reshape(Bs, Ls, nh, HEAD_DIM)
            outs["logf_fox_s"] = logf_s
        elif kind == 2:
            keep = state_pool.shape[1]
            prev_p = jnp.zeros((B, POOL_HALO, D), F32)
            prev_s = jnp.pad(state_pool, ((0, 0), (POOL_HALO - keep, 0), (0, 0)))
            y_p = _pool_mixer(x, prev_p, w_pool, pool_scale, row0=0, B=B, L=L, start=0, alpha=alpha)
            y_s = _pool_mixer(x, prev_s, w_pool, pool_scale, row0=Mp, B=Bs, L=Ls, start=past_len,
                              alpha=alpha)
            y = jnp.concatenate([y_p, y_s], axis=0)
            xs3 = x[Mp:].reshape(Bs, Ls, D)
            outs["pool_p"] = jnp.stack([x[(b + 1) * L - keep:(b + 1) * L] for b in range(B)])
            outs["pool_s"] = jnp.concatenate([state_pool, xs3], axis=1)[:, -keep:]
        else:
            keep = state_gdn_conv.shape[1]
            (qkv,) = _mm_plain(xb, w_in_gdn, [F32], col_off=0, n_cols=3 * D, name="gdn_qkv")
            (z,) = _mm_plain(xb, w_in_gdn, [F32], col_off=3 * D, n_cols=D, name="gdn_z")
            (ab,) = _mm_plain(xb, w_in_gdn[:, 4 * D:], [F32], name="gdn_ab")
            prev_p = jnp.zeros((B, CONV_HALO, 3 * D), F32)
            prev_s = jnp.pad(state_gdn_conv, ((0, 0), (CONV_HALO - keep, 0), (0, 0)))
            cn_p = _gdn_conv(qkv, prev_p, conv_gdn, row0=0, B=B, L=L, D=D)
            cn_s = _gdn_conv(qkv, prev_s, conv_gdn, row0=Mp, B=Bs, L=Ls, D=D)
            C = GDN_CHUNK
            o_p, S_p = _gdn_core(cn_p, z, ab, jnp.zeros((B, nh, HEAD_DIM, HEAD_DIM), F32),
                                 A_log_gdn, dt_bias_gdn, norm_gdn, B=B, L=L, valid_len=L, zrow0=0)
            padrows = lambda a: jnp.pad(a.reshape(Bs, Ls, -1), ((0, 0), (0, C - Ls), (0, 0))
                                        ).reshape(Bs * C, -1)
            o_s, S_s = _gdn_core(padrows(cn_s), padrows(z[Mp:]), padrows(ab[Mp:]), state_gdn_S,
                                 A_log_gdn, dt_bias_gdn, norm_gdn, B=Bs, L=C, valid_len=Ls, zrow0=0)
            o = jnp.concatenate([o_p, o_s.reshape(Bs, C, D)[:, :Ls].reshape(Ms, D)], axis=0)
            y = _mm_residual(o, w_o_gdn, x, alpha, "gdn_out")
            qkv_s = qkv[Mp:].reshape(Bs, Ls, 3 * D)
            outs["conv_p"] = jnp.stack([qkv[(b + 1) * L - keep:(b + 1) * L] for b in range(B)])
            outs["conv_s"] = jnp.concatenate([state_gdn_conv, qkv_s], axis=1)[:, -keep:]
            outs["S_p"] = S_p
            outs["S_s"] = S_s

        x, xb = _layer_norm(y, ln1_g[i], ln1_b[i])
        (h,) = _matmul(xb, w_mlp_up, layer=i, epilogue=_epi_relu2, out_dtypes=[BF16], name="mlp_up")
        y = _mm_residual(h, w_mlp_down, x, alpha, "mlp_down", layer=i, tn_target=1024, tk_target=1024)
        x, xb = _layer_norm(y, ln2_g[i], ln2_b[i])
        pcat = jnp.concatenate([p_prompt[i].reshape(Mp, -1), p_sample[i].reshape(Ms, -1)], axis=0)
        pdim = pcat.shape[1]
        tm, tn, _ = _mm_tiles(M, D, D, tn_target=256)
        x, xb = _matmul(
            xb, w_ple_gate, layer=i, epilogue=_epi_ple, out_dtypes=[F32, BF16], tn_target=256,
            name="ple",
            extras=[_tile_extra(x, tm, tn),
                    (pcat, (tm, pdim), lambda i_, j, k: (i_, 0)),
                    (w_ple_proj, (None, pdim, tn), lambda i_, j, k, layer=i: (layer, 0, j))])

    return (x[:Mp].reshape(B, L, D), x[Mp:].reshape(Bs, Ls, D),
            outs["k_diff_p"], outs["v_diff_p"], outs["k_diff_s"], outs["v_diff_s"],
            outs["k_fox_p"], outs["v_fox_p"], outs["logf_fox_p"],
            outs["k_fox_s"], outs["v_fox_s"], outs["logf_fox_s"],
            outs["pool_p"], outs["pool_s"], outs["conv_p"], outs["S_p"],
            outs["conv_s"], outs["S_s"])
```

```python
import functools
import math

import jax
import jax.numpy as jnp
from jax import lax
from jax.experimental import pallas as pl
from jax.experimental.pallas import tpu as pltpu

F32 = jnp.float32
BF16 = jnp.bfloat16

HEAD_DIM = 128
POOL_WINDOWS = (2, 4, 8, 16)
POOL_HALO = 16
CONV_HALO = 8
FLASH_COMPONENTS = 4
FLASH_BLOCK = 1024
GDN_CHUNK = 64
GDN_HEAD_GROUP = 32
GDN_STACK = 4
GDN_HI_ROUNDS = 1
ROPE_THETA = 10000.0
LN_EPS = 1e-5
RMS_EPS = 1e-5
N_MIXERS = 4
LANES = 128
SUBLANES = 8
VMEM_LIMIT = 56 * 1024 * 1024

_NT = (((1,), (1,)), ((), ()))
_TN = (((0,), (0,)), ((), ()))


def _params(*sem):
    return pltpu.CompilerParams(dimension_semantics=sem, vmem_limit_bytes=VMEM_LIMIT)


def _pick(dim, target, align):
    best = None
    for t in range(align, min(dim, target) + 1, align):
        if dim % t == 0:
            best = t
    return best if best is not None else dim


def _sigmoid(x):
    return 1.0 / (1.0 + jnp.exp(-x))


def _softplus(x):
    return jnp.maximum(x, 0.0) + jnp.log1p(jnp.exp(-jnp.abs(x)))


def _split3(a):
    hi = a.astype(BF16)
    r = a - hi.astype(F32)
    mid = r.astype(BF16)
    lo = (r - mid.astype(F32)).astype(BF16)
    return hi, mid, lo


def _split2(a):
    hi = a.astype(BF16)
    return hi, (a - hi.astype(F32)).astype(BF16)


def _dot(a, b):
    return jnp.dot(a, b, preferred_element_type=F32)


def _dot_hi(a, b):
    a1, a2 = _split2(a)
    b1, b2 = _split2(b)
    return _dot(a1, b1) + (_dot(a1, b2) + _dot(a2, b1))


def _dot_lo(a, b):
    return _dot(a.astype(BF16), b.astype(BF16))


def _tri_dot_hi(tri_bf16, x):
    x1, x2, x3 = _split3(x)
    return _dot(tri_bf16, x1) + _dot(tri_bf16, x2) + _dot(tri_bf16, x3)


def _mm_body(*refs, nk, n_extra, n_out, epilogue):
    x_ref, w_ref = refs[0], refs[1]
    extra = refs[2:2 + n_extra]
    outs = refs[2 + n_extra:2 + n_extra + n_out]
    if nk == 1:
        epilogue(_dot(x_ref[...].astype(BF16), w_ref[...].astype(BF16)), extra, outs)
        return
    acc_ref = refs[-1]
    k = pl.program_id(2)

    @pl.when(k == 0)
    def _():
        acc_ref[...] = jnp.zeros_like(acc_ref)

    acc_ref[...] += _dot(x_ref[...].astype(BF16), w_ref[...].astype(BF16))

    @pl.when(k == nk - 1)
    def _():
        epilogue(acc_ref[...], extra, outs)


def _mm_tiles(M, K, n_cols, tm_target=1376, tn_target=512, tk_target=None):
    if tk_target is None:
        tk_target = K if K <= 4096 else 2048
    return _pick(M, tm_target, 16), _pick(n_cols, tn_target, LANES), _pick(K, tk_target, LANES)


def _matmul(x, w, *, epilogue, out_dtypes, col_off=0, n_cols=None, extras=(), out_lane=None,
            layer=None, name="mm", **tile_kw):
    M, K = x.shape
    n_cols = w.shape[-1] - col_off if n_cols is None else n_cols
    tm, tn, tk = _mm_tiles(M, K, n_cols, **tile_kw)
    assert col_off % tn == 0
    joff = col_off // tn
    nk = K // tk
    grid = (M // tm, n_cols // tn, nk)
    xspec = (pl.BlockSpec((tm, tk), lambda i, j, k: (i, k), pipeline_mode=pl.Buffered(1))
             if nk == 1 else pl.BlockSpec((tm, tk), lambda i, j, k: (i, k)))
    if layer is None:
        wspec = pl.BlockSpec((tk, tn), lambda i, j, k: (k, j + joff))
    else:
        wspec = pl.BlockSpec((None, tk, tn), lambda i, j, k: (layer, k, j + joff))
    in_specs = [xspec, wspec]
    args = [x, w]
    for arr, bshape, imap in extras:
        in_specs.append(pl.BlockSpec(bshape, imap))
        args.append(arr)
    out_lane = [None] * len(out_dtypes) if out_lane is None else out_lane
    out_shape, out_specs = [], []
    for dt, dh in zip(out_dtypes, out_lane):
        if dh is None:
            out_shape.append(jax.ShapeDtypeStruct((M, n_cols), dt))
            out_specs.append(pl.BlockSpec((tm, tn), lambda i, j, k: (i, j)))
        else:
            out_shape.append(jax.ShapeDtypeStruct((M, n_cols // dh, dh), dt))
            out_specs.append(pl.BlockSpec((tm, tn // dh, dh), lambda i, j, k: (i, j, 0)))
    body = functools.partial(_mm_body, nk=nk, n_extra=len(extras), n_out=len(out_dtypes),
                             epilogue=epilogue)
    return pl.pallas_call(
        body, grid=grid, in_specs=in_specs, out_specs=out_specs, out_shape=out_shape,
        scratch_shapes=[pltpu.VMEM((tm, tn), F32)] if nk > 1 else [],
        compiler_params=_params("parallel", "parallel", "arbitrary"), name=name,
    )(*args)


def _put(o_ref, val):
    o_ref[...] = val.astype(o_ref.dtype).reshape(o_ref.shape)


def _epi_store(acc, extra, outs):
    for o in outs:
        _put(o, acc)


def _epi_rope(acc, extra, outs, *, scale):
    cos = extra[0][...]
    sin = extra[1][...]
    parts = []
    for c in range(acc.shape[1] // HEAD_DIM):
        xc = acc[:, c * HEAD_DIM:(c + 1) * HEAD_DIM]
        parts.append(xc * cos + pltpu.roll(xc, HEAD_DIM // 2, 1) * sin)
    r = jnp.concatenate(parts, axis=1)
    if scale != 1.0:
        r = r * scale
    for o in outs:
        _put(o, r)


def _epi_scale(acc, extra, outs, *, scale):
    for o in outs:
        _put(o, acc * scale)


def _epi_relu2(acc, extra, outs):
    h = jnp.maximum(acc, 0.0)
    _put(outs[0], h * h)


def _epi_residual(acc, extra, outs, *, alpha):
    _put(outs[0], alpha * extra[0][...] + acc)


def _epi_ple(acc, extra, outs):
    x_ref, p_ref, wp_ref = extra
    pp = _dot(p_ref[...].astype(BF16), wp_ref[...].astype(BF16))
    y = x_ref[...] + _sigmoid(acc) * pp
    for o in outs:
        _put(o, y)


def _epi_logsigmoid(acc, extra, outs):
    z = acc + extra[0][...]
    _put(outs[0], -(jnp.maximum(-z, 0.0) + jnp.log1p(jnp.exp(-jnp.abs(z)))))


def _tile_extra(arr, tm, tn):
    return (arr, (tm, tn), lambda i, j, k: (i, j))


def _mm_plain(x, w, out_dtypes, **kw):
    return _matmul(x, w, epilogue=_epi_store, out_dtypes=out_dtypes, **kw)


def _mm_residual(x, w, resid, alpha, name, layer=None, **tile_kw):
    M, N = resid.shape
    tm, tn, _ = _mm_tiles(M, x.shape[1], N, **tile_kw)
    return _matmul(x, w, epilogue=functools.partial(_epi_residual, alpha=alpha), out_dtypes=[F32],
                   extras=[_tile_extra(resid, tm, tn)], layer=layer, name=name, **tile_kw)[0]


def _ln_body(y_ref, g_ref, b_ref, o32_ref, o16_ref):
    y = y_ref[...]
    mu = jnp.mean(y, axis=-1, keepdims=True)
    d = y - mu
    var = jnp.mean(d * d, axis=-1, keepdims=True)
    out = d * lax.rsqrt(var + LN_EPS) * g_ref[...] + b_ref[...]
    o32_ref[...] = out
    o16_ref[...] = out.astype(BF16)


def _layer_norm(y, g, b):
    M, D = y.shape
    tm = _pick(M, 256, 16)
    row = pl.BlockSpec((tm, D), lambda i: (i, 0))
    vec = pl.BlockSpec((1, D), lambda i: (0, 0))
    return pl.pallas_call(
        _ln_body, grid=(M // tm,), in_specs=[row, vec, vec], out_specs=[row, row],
        out_shape=[jax.ShapeDtypeStruct((M, D), F32), jax.ShapeDtypeStruct((M, D), BF16)],
        compiler_params=_params("parallel"), name="layer_norm",
    )(y, g.reshape(1, D), b.reshape(1, D))


def _lambda_full(lam_ref, lambda_init):
    lam = lam_ref[...]
    e1 = jnp.exp(jnp.sum(lam[0:1] * lam[1:2], axis=-1, keepdims=True))
    e2 = jnp.exp(jnp.sum(lam[2:3] * lam[3:4], axis=-1, keepdims=True))
    return e1 - e2 + lambda_init


def _diff_finish(o0, o1, lam, subw, lambda_init):
    o = o0 - lam * o1
    o = o * lax.rsqrt(jnp.mean(o * o, axis=-1, keepdims=True) + RMS_EPS) * subw
    return o * (1.0 - lambda_init)


def _flash_body(*refs, mode, nk, tq, tk, nc, lambda_init):
    if mode == "diff":
        q_ref, k_ref, v_ref, lam_ref, subw_ref, o_ref, m_scr, l_scr, acc_scr = refs
    else:
        q_ref, k_ref, v_ref, cq_ref, ck_ref, o_ref, m_scr, l_scr, acc_scr = refs
    qi = pl.program_id(2)
    ki = pl.program_id(3)
    dv = acc_scr.shape[1]

    @pl.when(ki == 0)
    def _():
        m_scr[...] = jnp.full(m_scr.shape, -jnp.inf, F32)
        l_scr[...] = jnp.zeros(l_scr.shape, F32)
        acc_scr[...] = jnp.zeros(acc_scr.shape, F32)

    def update(on_diagonal):
        q = q_ref[...]
        k = k_ref[...]
        v = v_ref[...]
        if on_diagonal:
            mask = (lax.broadcasted_iota(jnp.int32, (tk, tq), 0)
                    <= lax.broadcasted_iota(jnp.int32, (tk, tq), 1))
        for c in range(nc):
            sl = slice(c * HEAD_DIM, (c + 1) * HEAD_DIM)
            s = lax.dot_general(k[:, sl], q[:, sl], _NT, preferred_element_type=F32)
            if mode == "fox":
                s = s + (cq_ref[0][c:c + 1, :] - ck_ref[0][:, c:c + 1])
            if on_diagonal:
                s = jnp.where(mask, s, -jnp.inf)
            m_old = m_scr[c]
            m_new = jnp.maximum(m_old, jnp.max(s, axis=0, keepdims=True))
            alpha = jnp.exp(m_old - m_new)
            p = jnp.exp(s - m_new)
            l_scr[c] = alpha * l_scr[c] + jnp.sum(p, axis=0, keepdims=True)
            vv = v[:, (c // 2) * dv:(c // 2 + 1) * dv] if mode == "diff" else v[:, sl]
            acc_scr[c] = alpha * acc_scr[c] + lax.dot_general(
                vv, p.astype(BF16), _TN, preferred_element_type=F32)
            m_scr[c] = m_new

    @pl.when(ki < qi)
    def _():
        update(False)

    @pl.when(ki == qi)
    def _():
        update(True)

    @pl.when(ki == nk - 1)
    def _():
        o = [acc_scr[c] / l_scr[c] for c in range(nc)]
        if mode == "diff":
            lam = _lambda_full(lam_ref, lambda_init)
            outs = []
            for c in range(0, nc, 2):
                d = o[c] - lam * o[c + 1]
                d = d * lax.rsqrt(jnp.mean(d * d, axis=0, keepdims=True) + RMS_EPS)
                outs.append(d.T * subw_ref[...] * (1.0 - lambda_init))
        else:
            outs = [oc.T for oc in o]
        o_ref[...] = jnp.concatenate(outs, axis=1).astype(o_ref.dtype)


def _flash_prompt(q, k, v, *, B, L, mode, extra, lambda_init=0.0):
    D = q.shape[1]
    nc = FLASH_COMPONENTS
    W = nc * HEAD_DIM
    assert D % W == 0
    tq = tk = _pick(L, FLASH_BLOCK, LANES)
    nq = L // tq
    grid = (B, D // W, nq, nq)
    qspec = pl.BlockSpec((tq, W), lambda b, h, qi, ki: (b * nq + qi, h))
    kspec = pl.BlockSpec((tk, W), lambda b, h, qi, ki: (b * nq + jnp.minimum(ki, qi), h))
    if mode == "diff":
        lam4, subw = extra
        especs = [pl.BlockSpec(lam4.shape, lambda b, h, qi, ki: (0, 0)),
                  pl.BlockSpec(subw.shape, lambda b, h, qi, ki: (0, 0))]
        dv = 2 * HEAD_DIM
    else:
        c_rows, c_cols = extra
        extra = (c_cols.reshape(D // W, nc, B * L),
                 c_rows.reshape(B * L, D // W, nc).transpose(1, 0, 2))
        especs = [pl.BlockSpec((1, nc, tq), lambda b, h, qi, ki: (h, 0, b * nq + qi)),
                  pl.BlockSpec((1, tk, nc), lambda b, h, qi, ki: (h, b * nq + jnp.minimum(ki, qi), 0))]
        dv = HEAD_DIM
    body = functools.partial(_flash_body, mode=mode, nk=nq, tq=tq, tk=tk, nc=nc,
                             lambda_init=lambda_init)
    return pl.pallas_call(
        body, grid=grid, in_specs=[qspec, kspec, kspec] + especs,
        out_specs=pl.BlockSpec((tq, W), lambda b, h, qi, ki: (b * nq + qi, h)),
        out_shape=jax.ShapeDtypeStruct((B * L, D), BF16),
        scratch_shapes=[pltpu.VMEM((nc, 1, tq), F32), pltpu.VMEM((nc, 1, tq), F32),
                        pltpu.VMEM((nc, dv, tq), F32)],
        compiler_params=_params("parallel", "parallel", "parallel", "arbitrary"),
        name="flash_" + mode,
    )(q, k, v, *extra)


def _decode_body(*refs, mode, npages, nh, ls, lambda_init):
    pt_ref = refs[0]
    if mode == "diff":
        (q_ref, kc_ref, vc_ref, kn_ref, vn_ref, lam_ref, subw_ref,
         o_ref, m_scr, l_scr, acc_scr) = refs[1:]
    else:
        (q_ref, kc_ref, vc_ref, kn_ref, vn_ref, cq_ref, ckc_ref, ckn_ref,
         o_ref, m_scr, l_scr, acc_scr) = refs[1:]
    del pt_ref
    p = pl.program_id(1)
    P = kc_ref.shape[1]
    dv = acc_scr.shape[-1]

    @pl.when(p == 0)
    def _():
        m_scr[...] = jnp.full(m_scr.shape, -jnp.inf, F32)
        l_scr[...] = jnp.zeros(l_scr.shape, F32)
        acc_scr[...] = jnp.zeros(acc_scr.shape, F32)

    def step(k_of, v_of, ck, mask):
        q = q_ref[0]
        probs = []
        for h in range(nh):
            s = lax.dot_general(q[:, h * HEAD_DIM:(h + 1) * HEAD_DIM], k_of(h), _NT,
                                preferred_element_type=F32)
            if mode == "fox":
                s = s + (cq_ref[0][:, h:h + 1] - ck[h:h + 1, :])
            if mask is not None:
                s = jnp.where(mask, s, -jnp.inf)
            m_old = m_scr[h]
            m_new = jnp.maximum(m_old, jnp.max(s, axis=-1, keepdims=True))
            alpha = jnp.exp(m_old - m_new)
            pr = jnp.exp(s - m_new)
            l_scr[h] = alpha * l_scr[h] + jnp.sum(pr, axis=-1, keepdims=True)
            m_scr[h] = m_new
            probs.append((alpha, pr.astype(BF16)))
        if mode == "diff":
            for hh in range(nh // 2):
                (a0, p0), (a1, p1) = probs[2 * hh], probs[2 * hh + 1]
                pv = _dot(jnp.concatenate([p0, p1], axis=0), v_of(hh))
                al = jnp.concatenate([a0, a1], axis=0)
                acc_scr[hh] = al * acc_scr[hh] + pv
        else:
            for h in range(nh):
                a0, p0 = probs[h]
                acc_scr[h] = a0 * acc_scr[h] + _dot(p0, v_of(h))

    @pl.when(p < npages)
    def _():
        kc = pltpu.einshape("khd->hkd", kc_ref[0].astype(BF16))
        vc = pltpu.einshape("khd->hkd", vc_ref[0].astype(BF16))
        ck = ckc_ref[0] if mode == "fox" else None
        step(lambda h: kc[h], lambda h: vc[h], ck, None)

    @pl.when(p == npages)
    def _():
        row = lax.broadcasted_iota(jnp.int32, (ls, P), 0)
        col = lax.broadcasted_iota(jnp.int32, (ls, P), 1)
        ck = ckn_ref[0] if mode == "fox" else None
        kn = kn_ref[0]
        vn = vn_ref[0]
        step(lambda h: kn[:, h * HEAD_DIM:(h + 1) * HEAD_DIM],
             lambda h: vn[:, h * dv:(h + 1) * dv], ck, col <= row)
        outs = []
        if mode == "diff":
            lam = _lambda_full(lam_ref, lambda_init)
            for hh in range(nh // 2):
                acc = acc_scr[hh]
                o0 = acc[:ls] / l_scr[2 * hh]
                o1 = acc[ls:] / l_scr[2 * hh + 1]
                outs.append(_diff_finish(o0, o1, lam, subw_ref[...], lambda_init))
        else:
            for h in range(nh):
                outs.append(acc_scr[h] / l_scr[h])
        o_ref[0] = jnp.concatenate(outs, axis=1).astype(o_ref.dtype)


def _decode_attention(q_s, k_new, v_new, cache_k, cache_v, page_table, *, mode, extra,
                      lambda_init=0.0):
    Bs, Ls, D = q_s.shape
    P = cache_k.shape[1]
    npages = page_table.shape[1]
    nh = D // HEAD_DIM
    pad = ((0, 0), (0, P - Ls), (0, 0))
    kn = jnp.pad(k_new, pad)
    vn = jnp.pad(v_new, pad)
    page = lambda b, p, pt: (pt[b, jnp.minimum(p, npages - 1)], 0, 0, 0)
    per_b = lambda b, p, pt: (b, 0, 0)
    in_specs = [pl.BlockSpec((1, Ls, D), per_b),
                pl.BlockSpec((1,) + cache_k.shape[1:], page),
                pl.BlockSpec((1,) + cache_v.shape[1:], page),
                pl.BlockSpec((1, P, D), per_b), pl.BlockSpec((1, P, D), per_b)]
    if mode == "diff":
        lam4, subw = extra
        in_specs += [pl.BlockSpec(lam4.shape, lambda b, p, pt: (0, 0)),
                     pl.BlockSpec(subw.shape, lambda b, p, pt: (0, 0))]
        acc_shape = (nh // 2, 2 * Ls, 2 * HEAD_DIM)
    else:
        cq, ckc, ckn = extra
        in_specs += [pl.BlockSpec((1, Ls, nh), per_b),
                     pl.BlockSpec((1, nh, P), lambda b, p, pt: (b, 0, jnp.minimum(p, npages - 1))),
                     pl.BlockSpec((1, nh, P), per_b)]
        acc_shape = (nh, Ls, HEAD_DIM)
    body = functools.partial(_decode_body, mode=mode, npages=npages, nh=nh, ls=Ls,
                             lambda_init=lambda_init)
    grid_spec = pltpu.PrefetchScalarGridSpec(
        num_scalar_prefetch=1, grid=(Bs, npages + 1), in_specs=in_specs,
        out_specs=pl.BlockSpec((1, Ls, D), per_b),
        scratch_shapes=[pltpu.VMEM((nh, Ls, 1), F32), pltpu.VMEM((nh, Ls, 1), F32),
                        pltpu.VMEM(acc_shape, F32)])
    return pl.pallas_call(
        body, grid_spec=grid_spec, out_shape=jax.ShapeDtypeStruct((Bs, Ls, D), BF16),
        compiler_params=_params("parallel", "arbitrary"), name="decode_" + mode,
    )(page_table, q_s, cache_k, cache_v, kn, vn, *extra)


def _cumsum_body(*refs, tb, per_step):
    x_refs = refs[1:1 + per_step]
    init_ref, o_ref, carry = refs[1 + per_step:]
    j = pl.program_id(1)

    @pl.when(j == 0)
    def _():
        carry[...] = init_ref[0]

    r = lax.broadcasted_iota(jnp.int32, (tb, tb), 0)
    c = lax.broadcasted_iota(jnp.int32, (tb, tb), 1)
    tri = (r >= c).astype(BF16)
    run = carry[...]
    for t, x_ref in enumerate(x_refs):
        out = _tri_dot_hi(tri, x_ref[0]) + run
        o_ref[0, t * tb:(t + 1) * tb, :] = out
        run = out[tb - 1:tb, :]
    carry[...] = run


def _cumsum_blocks(src, table, init):
    _, tb, H = src.shape
    S, nb = table.shape
    per_step = _pick(nb, 8, 1)
    xspec = lambda t_: pl.BlockSpec((1, tb, H), lambda s, j, t: (t[s, j * per_step + t_], 0, 0))
    grid_spec = pltpu.PrefetchScalarGridSpec(
        num_scalar_prefetch=1, grid=(S, nb // per_step),
        in_specs=[xspec(t_) for t_ in range(per_step)]
        + [pl.BlockSpec((1, 1, H), lambda s, j, t: (s, 0, 0))],
        out_specs=pl.BlockSpec((1, per_step * tb, H), lambda s, j, t: (s, j, 0)),
        scratch_shapes=[pltpu.VMEM((1, H), F32)])
    return pl.pallas_call(
        functools.partial(_cumsum_body, tb=tb, per_step=per_step), grid_spec=grid_spec,
        out_shape=jax.ShapeDtypeStruct((S, nb * tb, H), F32),
        compiler_params=_params("parallel", "arbitrary"), name="logf_cumsum",
    )(table, *([src] * per_step), init)


def _pool_body(x_ref, prev_ref, w_ref, scale_ref, o_ref, carry, *, tl, start, alpha):
    g = pl.program_id(1)
    t = pl.program_id(2)

    @pl.when(t == 0)
    def _():
        carry[...] = prev_ref[0]

    x = x_ref[...]
    xe = jnp.concatenate([carry[...], x], axis=0)
    s2 = xe + pltpu.roll(xe, 1, 0)
    s4 = s2 + pltpu.roll(s2, 2, 0)
    s8 = s4 + pltpu.roll(s4, 4, 0)
    s16 = s8 + pltpu.roll(s8, 8, 0)
    win = jnp.where(g == 0, s2, jnp.where(g == 1, s4, jnp.where(g == 2, s8, s16)))[POOL_HALO:]
    width = jnp.left_shift(2, g)
    pos = start + t * tl + lax.broadcasted_iota(jnp.int32, (tl, 1), 0)
    cnt = jnp.minimum(pos + 1, width).astype(F32)
    mixed = win / cnt - x
    y = _dot(mixed.astype(BF16), w_ref[0].astype(BF16)) * scale_ref[...]
    o_ref[...] = alpha * x + y
    carry[...] = xe[tl:tl + POOL_HALO]


def _pool_mixer(x, prev, w_pool, pool_scale, *, row0, B, L, start, alpha):
    D = x.shape[1]
    ng = len(POOL_WINDOWS)
    G = D // ng
    tl = _pick(L, 512, SUBLANES)
    nt = L // tl
    r0 = row0 // tl
    assert row0 % tl == 0
    body = functools.partial(_pool_body, tl=tl, start=start, alpha=alpha)
    return pl.pallas_call(
        body, grid=(B, ng, nt),
        in_specs=[pl.BlockSpec((tl, G), lambda b, g, t: (r0 + b * nt + t, g)),
                  pl.BlockSpec((1, POOL_HALO, G), lambda b, g, t: (b, 0, g)),
                  pl.BlockSpec((1, G, G), lambda b, g, t: (g, 0, 0)),
                  pl.BlockSpec((1, G), lambda b, g, t: (0, g))],
        out_specs=pl.BlockSpec((tl, G), lambda b, g, t: (b * nt + t, g)),
        out_shape=jax.ShapeDtypeStruct((B * L, D), F32),
        scratch_shapes=[pltpu.VMEM((POOL_HALO, G), F32)],
        compiler_params=_params("parallel", "parallel", "arbitrary"), name="pool_mixer",
    )(x, prev, w_pool, pool_scale.reshape(1, D))


def _conv_body(x_ref, prev_ref, w_ref, o_ref, carry, *, tl, ncb_part, qscale):
    cb = pl.program_id(1)
    t = pl.program_id(2)

    @pl.when(t == 0)
    def _():
        carry[...] = prev_ref[0]

    x = x_ref[...]
    w = w_ref[...]
    xe = jnp.concatenate([carry[...], x], axis=0)
    x1 = pltpu.roll(xe, 1, 0)[CONV_HALO:]
    x2 = pltpu.roll(xe, 2, 0)[CONV_HALO:]
    x3 = pltpu.roll(xe, 3, 0)[CONV_HALO:]
    conv = x3 * w[0:1] + x2 * w[1:2] + x1 * w[2:3] + x * w[3:4]
    act = conv * _sigmoid(conv)
    part = cb // ncb_part
    scale = jnp.where(part == 0, qscale, 1.0)
    pieces = []
    for c in range(act.shape[1] // HEAD_DIM):
        a = act[:, c * HEAD_DIM:(c + 1) * HEAD_DIM]
        n = a * lax.rsqrt(jnp.sum(a * a, axis=-1, keepdims=True) + 1e-6) * scale
        pieces.append(jnp.where(part < 2, n, a))
    o_ref[...] = jnp.concatenate(pieces, axis=1)
    carry[...] = xe[tl:tl + CONV_HALO]


def _gdn_conv(qkv, prev, conv_w, *, row0, B, L, D):
    assert conv_w.shape[0] == 4
    C3 = qkv.shape[1]
    tc = _pick(D, 512, LANES)
    tl = _pick(L, 512, SUBLANES)
    nt = L // tl
    r0 = row0 // tl
    assert row0 % tl == 0
    body = functools.partial(_conv_body, tl=tl, ncb_part=D // tc, qscale=HEAD_DIM ** -0.5)
    return pl.pallas_call(
        body, grid=(B, C3 // tc, nt),
        in_specs=[pl.BlockSpec((tl, tc), lambda b, c, t: (r0 + b * nt + t, c)),
                  pl.BlockSpec((1, CONV_HALO, tc), lambda b, c, t: (b, 0, c)),
                  pl.BlockSpec((4, tc), lambda b, c, t: (0, c))],
        out_specs=pl.BlockSpec((tl, tc), lambda b, c, t: (b * nt + t, c)),
        out_shape=jax.ShapeDtypeStruct((B * L, C3), F32),
        scratch_shapes=[pltpu.VMEM((CONV_HALO, tc), F32)],
        compiler_params=_params("parallel", "parallel", "arbitrary"), name="gdn_conv",
    )(qkv, prev, conv_w)


def _gdn_body(q_ref, k_ref, v_ref, z_ref, ab_ref, alog_ref, dtb_ref, nw_ref, s0_ref,
              o_ref, s_ref, gct_scr, *, C, HG, G4, nh, valid_len):
    hg = pl.program_id(1)
    c = pl.program_id(2)

    @pl.when(c == 0)
    def _():
        s_ref[...] = s0_ref[...]

    ab = ab_ref[...]
    row = c * C + lax.broadcasted_iota(jnp.int32, (C, 1), 0)
    valid = row < valid_len
    beta = jnp.where(valid, _sigmoid(ab[:, :nh]), 0.0)
    g = jnp.where(valid, -jnp.exp(alog_ref[...]) * _softplus(ab[:, nh:] + dtb_ref[...]), 0.0)
    ii = lax.broadcasted_iota(jnp.int32, (C, C), 0)
    jj = lax.broadcasted_iota(jnp.int32, (C, C), 1)
    gc = _tri_dot_hi((ii >= jj).astype(BF16), g)
    g_last = gc[C - 1:C, :]
    e_gc = jnp.exp(gc)
    e_kd = jnp.exp(g_last - gc)
    e_end = jnp.exp(g_last)
    gpad = jnp.concatenate([gc, jnp.zeros((C, LANES - nh), F32)], axis=1)
    gpad = jnp.concatenate([gpad, jnp.zeros((LANES - C, LANES), F32)], axis=0)
    gct_scr[...] = gpad.T
    lane_h = lax.broadcasted_iota(jnp.int32, (1, nh), 1)
    nw = nw_ref[...]
    rounds = max(int(math.ceil(math.log2(C))) - 1, 0)

    W4 = G4 * C
    ri = lax.broadcasted_iota(jnp.int32, (W4, W4), 0)
    ci = lax.broadcasted_iota(jnp.int32, (W4, W4), 1)
    same = (ri // C) == (ci // C)
    incl = same & (ri >= ci)
    strict = same & (ri > ci)
    eye = (ri == ci).astype(F32)
    rows = lambda a, t: a[t * C:(t + 1) * C]

    for grp in range(HG // G4):
        cols, grow, eend = [], [], []
        for t in range(G4):
            head = hg * HG + grp * G4 + t
            sel = lane_h == head
            col = lambda a: jnp.sum(jnp.where(sel, a, 0.0), axis=1, keepdims=True)
            cols.append((col(gc), col(beta), col(e_gc), col(e_kd)))
            eend.append(col(e_end))
            grow.append(gct_scr[pl.ds(head, 1), :][:, :C])
        stack = lambda idx: jnp.concatenate([cl[idx] for cl in cols], axis=0)
        gcol, bcol, egc, ekd = stack(0), stack(1), stack(2), stack(3)
        grow = jnp.concatenate(grow, axis=1)
        sls = [slice((grp * G4 + t) * HEAD_DIM, (grp * G4 + t + 1) * HEAD_DIM) for t in range(G4)]
        q = jnp.concatenate([q_ref[:, sl] for sl in sls], axis=0)
        k = jnp.concatenate([k_ref[:, sl] for sl in sls], axis=0)
        v = jnp.concatenate([v_ref[:, sl] for sl in sls], axis=0)
        kb = k.astype(BF16)
        decay = jnp.exp(jnp.where(incl, gcol - grow, -jnp.inf))
        kk = lax.dot_general(kb, kb, _NT, preferred_element_type=F32)
        aqk = lax.dot_general(q.astype(BF16), kb, _NT, preferred_element_type=F32) * decay
        A = jnp.where(strict, bcol * kk * decay, 0.0)
        X = eye - A
        Pw = A
        for rd in range(rounds):
            mm = _dot_hi if rd < GDN_HI_ROUNDS else _dot_lo
            Pw = mm(Pw, Pw)
            X = X + mm(X, Pw)
        rhs = jnp.concatenate([bcol * v, (bcol * egc) * k], axis=1)
        sol = _dot_hi(X, rhs)
        u = sol[:, :HEAD_DIM]
        wkqg = jnp.concatenate([sol[:, HEAD_DIM:], q * egc], axis=1).astype(BF16)
        kd = (k * ekd).astype(BF16)
        S = [s_ref[0, grp * G4 + t] for t in range(G4)]
        w_parts, o_parts = [], []
        for t in range(G4):
            Sb = S[t].astype(BF16)
            both = _dot(jnp.concatenate([rows(wkqg[:, :HEAD_DIM], t), rows(wkqg[:, HEAD_DIM:], t)],
                                        axis=0), Sb)
            w_parts.append(rows(u, t) - both[:C])
            o_parts.append(both[C:])
        w = jnp.concatenate(w_parts, axis=0)
        wb = w.astype(BF16)
        o = jnp.concatenate(o_parts, axis=0) + _dot(aqk.astype(BF16), wb)
        for t in range(G4):
            s_ref[0, grp * G4 + t] = eend[t] * S[t] + lax.dot_general(
                rows(kd, t), rows(wb, t), _TN, preferred_element_type=F32)
            ot = rows(o, t)
            zz = z_ref[:, sls[t]]
            on = ot * lax.rsqrt(jnp.mean(ot * ot, axis=-1, keepdims=True) + RMS_EPS) * nw
            o_ref[:, sls[t]] = (on * (zz * _sigmoid(zz))).astype(o_ref.dtype)


def _gdn_core(qkvn, z, ab, S0, A_log, dt_bias, norm_w, *, B, L, valid_len, zrow0):
    D = qkvn.shape[1] // 3
    nh = D // HEAD_DIM
    C = GDN_CHUNK
    HG = min(GDN_HEAD_GROUP, nh)
    G4 = min(GDN_STACK, HG)
    assert HG % G4 == 0 and nh % HG == 0
    W = HG * HEAD_DIM
    nc = L // C
    npart = D // W
    zr0 = zrow0 // C
    assert L % C == 0 and zrow0 % C == 0
    body = functools.partial(_gdn_body, C=C, HG=HG, G4=G4, nh=nh, valid_len=valid_len)
    vec = lambda n: pl.BlockSpec((1, n), lambda b, h, c: (0, 0))
    return pl.pallas_call(
        body, grid=(B, nh // HG, nc),
        in_specs=[pl.BlockSpec((C, W), lambda b, h, c: (b * nc + c, h)),
                  pl.BlockSpec((C, W), lambda b, h, c: (b * nc + c, npart + h)),
                  pl.BlockSpec((C, W), lambda b, h, c: (b * nc + c, 2 * npart + h)),
                  pl.BlockSpec((C, W), lambda b, h, c: (zr0 + b * nc + c, h)),
                  pl.BlockSpec((C, 2 * nh), lambda b, h, c: (zr0 + b * nc + c, 0)),
                  vec(nh), vec(nh), vec(HEAD_DIM),
                  pl.BlockSpec((1, HG, HEAD_DIM, HEAD_DIM), lambda b, h, c: (b, h, 0, 0))],
        out_specs=[pl.BlockSpec((C, W), lambda b, h, c: (b * nc + c, h)),
                   pl.BlockSpec((1, HG, HEAD_DIM, HEAD_DIM), lambda b, h, c: (b, h, 0, 0))],
        out_shape=[jax.ShapeDtypeStruct((B * L, D), BF16),
                   jax.ShapeDtypeStruct((B, nh, HEAD_DIM, HEAD_DIM), F32)],
        scratch_shapes=[pltpu.VMEM((LANES, LANES), F32)],
        compiler_params=_params("parallel", "parallel", "arbitrary"), name="gdn_core",
    )(qkvn, qkvn, qkvn, z, ab, A_log.reshape(1, nh), dt_bias.reshape(1, nh),
      norm_w.reshape(1, HEAD_DIM), S0)


def _rope_tables(B, L, Bs, Ls, past_len):
    pos = jnp.concatenate([jnp.tile(jnp.arange(L), B), jnp.tile(past_len + jnp.arange(Ls), Bs)])
    inv = ROPE_THETA ** (-jnp.arange(0, HEAD_DIM, 2, dtype=F32) / HEAD_DIM)
    ang = pos.astype(F32)[:, None] * inv[None, :]
    cos = jnp.cos(ang)
    sin = jnp.sin(ang)
    return jnp.concatenate([cos, cos], axis=1), jnp.concatenate([-sin, sin], axis=1)


def _kv_proj(xb, w, col_off, D, dh, epilogue, extras, name):
    tn = SUBLANES * dh
    tm_target = 1376 if tn <= 1024 else 688
    return _matmul(xb, w, col_off=col_off, n_cols=D, epilogue=epilogue, extras=extras,
                   out_dtypes=[F32, BF16], out_lane=[dh, None], name=name,
                   tm_target=tm_target, tn_target=tn, tk_target=1024)


def kernel(x_prompt, x_sample, cache_k_diff, cache_v_diff, cache_k_fox, cache_v_fox, cache_logf_fox, state_pool, state_gdn_conv, state_gdn_S, page_table, p_prompt, p_sample, ln1_g, ln1_b, ln2_g, ln2_b, w_mlp_up, w_mlp_down, w_ple_gate, w_ple_proj, w_qkv_diff, lam_q1, lam_k1, lam_q2, lam_k2, subln_diff, w_o_diff, w_in_fox, b_f_fox, w_o_fox, w_pool, pool_scale, w_in_gdn, conv_gdn, A_log_gdn, dt_bias_gdn, norm_gdn, w_o_gdn):
    B, L, D = x_prompt.shape
    Bs, Ls, _ = x_sample.shape
    depth = ln1_g.shape[0]
    Mp, Ms = B * L, Bs * Ls
    M = Mp + Ms
    nh = D // HEAD_DIM
    P = cache_k_diff.shape[1]
    npages = page_table.shape[1]
    past_len = npages * P
    alpha = (2 * depth) ** 0.25
    qk_scale = HEAD_DIM ** -0.5

    x = jnp.concatenate([x_prompt.reshape(Mp, D), x_sample.reshape(Ms, D)], axis=0)
    xb = x.astype(BF16)
    outs = {}

    def rope_extras(tm):
        cos, sin = _rope_tables(B, L, Bs, Ls, past_len)
        return [(cos, (tm, HEAD_DIM), lambda i_, j, k: (i_, 0)),
                (sin, (tm, HEAD_DIM), lambda i_, j, k: (i_, 0))]

    for i in range(depth):
        kind = i % N_MIXERS
        if kind == 0:
            lambda_init = 0.8 - 0.6 * math.exp(-0.3 * i)
            dv = 2 * HEAD_DIM
            (q,) = _matmul(xb, w_qkv_diff, col_off=0, n_cols=D, out_dtypes=[BF16],
                           epilogue=functools.partial(_epi_rope, scale=qk_scale),
                           extras=rope_extras(_mm_tiles(M, D, D)[0]), name="diff_q")
            tmk = _mm_tiles(M, D, D, tn_target=SUBLANES * HEAD_DIM, tk_target=1024)[0]
            k4, kb = _kv_proj(xb, w_qkv_diff, D, D, HEAD_DIM,
                              functools.partial(_epi_rope, scale=1.0), rope_extras(tmk), "diff_k")
            v4, vb = _kv_proj(xb, w_qkv_diff, 2 * D, D, dv, _epi_store, (), "diff_v")
            lam4 = jnp.stack([lam_q1, lam_k1, lam_q2, lam_k2]).astype(F32)
            extra = (lam4, subln_diff.reshape(1, dv))
            o_p = _flash_prompt(q, kb, vb, B=B, L=L, mode="diff", extra=extra,
                                lambda_init=lambda_init)
            o_s = _decode_attention(q[Mp:].reshape(Bs, Ls, D), kb[Mp:].reshape(Bs, Ls, D),
                                    vb[Mp:].reshape(Bs, Ls, D), cache_k_diff, cache_v_diff,
                                    page_table, mode="diff", extra=extra, lambda_init=lambda_init)
            o = jnp.concatenate([o_p, o_s.reshape(Ms, D)], axis=0)
            y = _mm_residual(o, w_o_diff, x, alpha, "diff_out")
            outs["k_diff_p"] = k4[:Mp].reshape(B, L, nh, HEAD_DIM)
            outs["v_diff_p"] = v4[:Mp].reshape(B, L, D // dv, dv)
            outs["k_diff_s"] = k4[Mp:].reshape(Bs, Ls, nh, HEAD_DIM)
            outs["v_diff_s"] = v4[Mp:].reshape(Bs, Ls, D // dv, dv)
        elif kind == 1:
            (q,) = _matmul(xb, w_in_fox, col_off=0, n_cols=D, out_dtypes=[BF16],
                           epilogue=functools.partial(_epi_scale, scale=qk_scale), name="fox_q")
            k4, kb = _kv_proj(xb, w_in_fox, D, D, HEAD_DIM, _epi_store, (), "fox_k")
            v4, vb = _kv_proj(xb, w_in_fox, 2 * D, D, HEAD_DIM, _epi_store, (), "fox_v")
            (logf,) = _matmul(xb, w_in_fox[:, 3 * D:], epilogue=_epi_logsigmoid, out_dtypes=[F32],
                              extras=[(b_f_fox.reshape(1, nh), (1, nh), lambda i_, j, k: (0, 0))],
                              name="fox_logf")
            tb = P
            tbl_p = jnp.arange(Mp // tb, dtype=jnp.int32).reshape(B, L // tb)
            c_p = _cumsum_blocks(logf[:Mp].reshape(Mp // tb, tb, nh), tbl_p,
                                 jnp.zeros((B, 1, nh), F32)).reshape(Mp, nh)
            c_c = _cumsum_blocks(cache_logf_fox, page_table, jnp.zeros((Bs, 1, nh), F32))
            logf_s = logf[Mp:].reshape(Bs, Ls, nh)
            c_n = _cumsum_blocks(jnp.pad(logf_s, ((0, 0), (0, tb - Ls), (0, 0))),
                                 jnp.arange(Bs, dtype=jnp.int32).reshape(Bs, 1),
                                 c_c[:, past_len - 1:past_len, :])
            o_p = _flash_prompt(q, kb, vb, B=B, L=L, mode="fox", extra=(c_p, c_p.T))
            extra = (c_n[:, :Ls, :], c_c.transpose(0, 2, 1), c_n.transpose(0, 2, 1))
            o_s = _decode_attention(q[Mp:].reshape(Bs, Ls, D), kb[Mp:].reshape(Bs, Ls, D),
                                    vb[Mp:].reshape(Bs, Ls, D), cache_k_fox, cache_v_fox,
                                    page_table, mode="fox", extra=extra)
            o = jnp.concatenate([o_p, o_s.reshape(Ms, D)], axis=0)
            y = _mm_residual(o, w_o_fox, x, alpha, "fox_out")
            outs["k_fox_p"] = k4[:Mp].reshape(B, L, nh, HEAD_DIM)
            outs["v_fox_p"] = v4[:Mp].reshape(B, L, nh, HEAD_DIM)
            outs["logf_fox_p"] = logf[:Mp].reshape(B, L, nh)
            outs["k_fox_s"] = k4[Mp:].reshape(Bs, Ls, nh, HEAD_DIM)
            outs["v_fox_s"] = v4[Mp:].reshape(Bs, Ls, nh, HEAD_DIM)
            outs["logf_fox_s"] = logf_s
        elif kind == 2:
            keep = state_pool.shape[1]
            prev_p = jnp.zeros((B, POOL_HALO, D), F32)
            prev_s = jnp.pad(state_pool, ((0, 0), (POOL_HALO - keep, 0), (0, 0)))
            y_p = _pool_mixer(x, prev_p, w_pool, pool_scale, row0=0, B=B, L=L, start=0, alpha=alpha)
            y_s = _pool_mixer(x, prev_s, w_pool, pool_scale, row0=Mp, B=Bs, L=Ls, start=past_len,
                              alpha=alpha)
            y = jnp.concatenate([y_p, y_s], axis=0)
            xs3 = x[Mp:].reshape(Bs, Ls, D)
            outs["pool_p"] = jnp.stack([x[(b + 1) * L - keep:(b + 1) * L] for b in range(B)])
            outs["pool_s"] = jnp.concatenate([state_pool, xs3], axis=1)[:, -keep:]
        else:
            keep = state_gdn_conv.shape[1]
            (qkv,) = _mm_plain(xb, w_in_gdn, [F32], col_off=0, n_cols=3 * D, name="gdn_qkv")
            (z,) = _mm_plain(xb, w_in_gdn, [F32], col_off=3 * D, n_cols=D, name="gdn_z")
            (ab,) = _mm_plain(xb, w_in_gdn[:, 4 * D:], [F32], name="gdn_ab")
            prev_p = jnp.zeros((B, CONV_HALO, 3 * D), F32)
            prev_s = jnp.pad(state_gdn_conv, ((0, 0), (CONV_HALO - keep, 0), (0, 0)))
            cn_p = _gdn_conv(qkv, prev_p, conv_gdn, row0=0, B=B, L=L, D=D)
            cn_s = _gdn_conv(qkv, prev_s, conv_gdn, row0=Mp, B=Bs, L=Ls, D=D)
            C = GDN_CHUNK
            o_p, S_p = _gdn_core(cn_p, z, ab, jnp.zeros((B, nh, HEAD_DIM, HEAD_DIM), F32),
                                 A_log_gdn, dt_bias_gdn, norm_gdn, B=B, L=L, valid_len=L, zrow0=0)
            padrows = lambda a: jnp.pad(a.reshape(Bs, Ls, -1), ((0, 0), (0, C - Ls), (0, 0))
                                        ).reshape(Bs * C, -1)
            o_s, S_s = _gdn_core(padrows(cn_s), padrows(z[Mp:]), padrows(ab[Mp:]), state_gdn_S,
                                 A_log_gdn, dt_bias_gdn, norm_gdn, B=Bs, L=C, valid_len=Ls, zrow0=0)
            o = jnp.concatenate([o_p, o_s.reshape(Bs, C, D)[:, :Ls].reshape(Ms, D)], axis=0)
            y = _mm_residual(o, w_o_gdn, x, alpha, "gdn_out")
            qkv_s = qkv[Mp:].reshape(Bs, Ls, 3 * D)
            outs["conv_p"] = jnp.stack([qkv[(b + 1) * L - keep:(b + 1) * L] for b in range(B)])
            outs["conv_s"] = jnp.concatenate([state_gdn_conv, qkv_s], axis=1)[:, -keep:]
            outs["S_p"] = S_p
            outs["S_s"] = S_s

        x, xb = _layer_norm(y, ln1_g[i], ln1_b[i])
        (h,) = _matmul(xb, w_mlp_up, layer=i, epilogue=_epi_relu2, out_dtypes=[BF16], name="mlp_up")
        y = _mm_residual(h, w_mlp_down, x, alpha, "mlp_down", layer=i, tn_target=1024, tk_target=1024)
        x, xb = _layer_norm(y, ln2_g[i], ln2_b[i])
        pcat = jnp.concatenate([p_prompt[i].reshape(Mp, -1), p_sample[i].reshape(Ms, -1)], axis=0)
        pdim = pcat.shape[1]
        tm, tn, _ = _mm_tiles(M, D, D, tn_target=256)
        x, xb = _matmul(
            xb, w_ple_gate, layer=i, epilogue=_epi_ple, out_dtypes=[F32, BF16], tn_target=256,
            name="ple",
            extras=[_tile_extra(x, tm, tn),
                    (pcat, (tm, pdim), lambda i_, j, k: (i_, 0)),
                    (w_ple_proj, (None, pdim, tn), lambda i_, j, k, layer=i: (layer, 0, j))])

    return (x[:Mp].reshape(B, L, D), x[Mp:].reshape(Bs, Ls, D),
            outs["k_diff_p"], outs["v_diff_p"], outs["k_diff_s"], outs["v_diff_s"],
            outs["k_fox_p"], outs["v_fox_p"], outs["logf_fox_p"],
            outs["k_fox_s"], outs["v_fox_s"], outs["logf_fox_s"],
            outs["pool_p"], outs["pool_s"], outs["conv_p"], outs["S_p"],
            outs["conv_s"], outs["S_s"])
```
